```python
import math
import jax, jax.numpy as jnp
from jax import lax
import numpy as np

D_MODEL = 2048
BATCH = 2
SEQ = 16384
DEPTH = 2
DEC_BATCH = 4
DEC_SEQ = 8192
PAST_LEN = 128

GRID_W = 64
HEAD_DIM = 64
ROT_DIM = HEAD_DIM // 4
ROPE_THETA = 500000.0
EPS = 1e-6
Q_BLOCK = 128
HALF = 0.5

FNET_GROUPS = 4
FNET_GROUP_DIM = 128
FNET_WIDTH = FNET_GROUPS * FNET_GROUP_DIM
NAT_HEADS = 8
NAT_WIN_H = 8
NAT_WIN_W = 16
NAT_WIDTH = NAT_HEADS * HEAD_DIM
DIFF_HEADS = 4
DIFF_QK_WIDTH = DIFF_HEADS * 2 * HEAD_DIM
DIFF_V_DIM = 2 * HEAD_DIM
DIFF_WIDTH = DIFF_HEADS * DIFF_V_DIM
DIL_PATTERNS = ((128, 1), (512, 4), (2048, 16))
DIL_GROUPS = len(DIL_PATTERNS)
DIL_HEADS = 8
DIL_QKV_WIDTH = DIL_GROUPS * DIL_HEADS * HEAD_DIM
DIL_WIDTH = DIL_HEADS * HEAD_DIM

N_BRANCH = 4
BRANCH_WIDTH = 512
MIX_COLS = FNET_WIDTH + 3 * NAT_WIDTH + 2 * DIFF_QK_WIDTH + DIFF_WIDTH + 3 * DIL_QKV_WIDTH
IN_COLS = MIX_COLS + N_BRANCH * D_MODEL
N_MOD = 9
D_FF = ((8 * D_MODEL // 3 + 63) // 64) * 64

kernel_name = "hybrid_gated_fourier_nat_diff_dilated_encoder"


def _rmsnorm(x, g):
    xf = x.astype(jnp.float32)
    y = xf * lax.rsqrt(jnp.mean(xf * xf, axis=-1, keepdims=True) + EPS)
    return (y * g).astype(x.dtype)


def _modulate(h, shift, scale):
    return h * (1.0 + scale) + shift


def _swiglu(h, w_in, w_out):
    a, b = jnp.split(h @ w_in, 2, axis=-1)
    return (jax.nn.silu(a) * b) @ w_out


def _rope_tables(T):
    inv = ROPE_THETA ** (-jnp.arange(0, ROT_DIM, 2, dtype=jnp.float32) / ROT_DIM)
    ang = jnp.arange(T, dtype=jnp.float32)[:, None] * inv[None, :]
    return jnp.cos(ang), jnp.sin(ang)


def _apply_rope(x, cos, sin):
    xr = x[..., :ROT_DIM].astype(jnp.float32)
    x1, x2 = xr[..., :ROT_DIM // 2], xr[..., ROT_DIM // 2:]
    c = cos[None, :, None, :]
    s = sin[None, :, None, :]
    rot = jnp.concatenate([x1 * c - x2 * s, x2 * c + x1 * s], axis=-1).astype(x.dtype)
    return jnp.concatenate([rot, x[..., ROT_DIM:]], axis=-1)


def _split_cols(z):
    sizes = [FNET_WIDTH, NAT_WIDTH, NAT_WIDTH, NAT_WIDTH,
             DIFF_QK_WIDTH, DIFF_QK_WIDTH, DIFF_WIDTH,
             DIL_QKV_WIDTH, DIL_QKV_WIDTH, DIL_QKV_WIDTH,
             N_BRANCH * D_MODEL]
    idx = np.cumsum(sizes)[:-1].tolist()
    return jnp.split(z, idx, axis=-1)


def _fourier_mix(u):
    B, T, G, C = u.shape
    f = jnp.fft.fft2(u.astype(jnp.float32), axes=(1, 3), norm="ortho")
    return jnp.real(f).astype(u.dtype).reshape(B, T, G * C)


def _neighborhood_attention(q, k, v, rel_bias):
    B, T, H, dh = q.shape
    rows = T // GRID_W
    kh = min(NAT_WIN_H, rows)
    qg = q.reshape(B, rows, GRID_W, H, dh)
    kg = k.reshape(B, rows, GRID_W, H, dh)
    vg = v.reshape(B, rows, GRID_W, H, dh)
    col = jnp.arange(GRID_W)
    col_start = jnp.clip(col - NAT_WIN_W // 2, 0, GRID_W - NAT_WIN_W)
    col_idx = col_start[:, None] + jnp.arange(NAT_WIN_W)[None, :]
    col_off = col_idx - col[:, None] + (NAT_WIN_W - 1)
    scale = dh ** -0.5

    def row_block(r):
        rs = jnp.clip(r - NAT_WIN_H // 2, 0, rows - kh)
        k_slab = lax.dynamic_slice_in_dim(kg, rs, kh, axis=1)
        v_slab = lax.dynamic_slice_in_dim(vg, rs, kh, axis=1)
        k_sel = k_slab[:, :, col_idx]
        v_sel = v_slab[:, :, col_idx]
        q_row = lax.dynamic_index_in_dim(qg, r, axis=1, keepdims=False)
        s = jnp.einsum('bwhd,bawjhd->bhwaj', q_row, k_sel,
                       preferred_element_type=jnp.float32) * scale
        row_off = rs + jnp.arange(kh) - r + (NAT_WIN_H - 1)
        bias = rel_bias[:, row_off][:, :, col_off]
        s = s + jnp.transpose(bias, (0, 2, 1, 3))[None].astype(jnp.float32)
        p = jax.nn.softmax(s.reshape(B, H, GRID_W, kh * NAT_WIN_W), axis=-1)
        p = p.reshape(B, H, GRID_W, kh, NAT_WIN_W).astype(v.dtype)
        return jnp.einsum('bhwaj,bawjhd->bwhd', p, v_sel)

    out = lax.map(row_block, jnp.arange(rows))
    return jnp.transpose(out, (1, 0, 2, 3, 4)).reshape(B, T, H * dh)


def _diff_attention(q, k, v, lam, lam_init, ln_g):
    B, T, H, _, dh = q.shape
    scale = dh ** -0.5
    nb = T // Q_BLOCK
    qb = jnp.transpose(q.reshape(B, nb, Q_BLOCK, H, 2, dh), (1, 0, 2, 3, 4, 5))

    def block(qblk):
        s = jnp.einsum('bqhcd,bkhcd->bchqk', qblk, k,
                       preferred_element_type=jnp.float32) * scale
        p = jax.nn.softmax(s, axis=-1)
        a = p[:, 0] - lam * p[:, 1]
        return jnp.einsum('bhqk,bkhe->bqhe', a.astype(v.dtype), v)

    o = lax.map(block, qb)
    o = jnp.transpose(o, (1, 0, 2, 3, 4)).reshape(B, T, H, 2 * dh)
    o = _rmsnorm(o, ln_g) * (1.0 - lam_init)
    return o.reshape(B, T, H * 2 * dh)


def _dilated_group(q, k, v, window, dil):
    B, T, H, dh = q.shape
    side = window // (2 * dil)
    offs = dil * jnp.arange(-side, side + 1)
    scale = dh ** -0.5
    nb = T // Q_BLOCK
    qb = jnp.transpose(q.reshape(B, nb, Q_BLOCK, H, dh), (1, 0, 2, 3, 4))
    starts = jnp.arange(nb, dtype=jnp.int32) * Q_BLOCK

    def block(args):
        qblk, start = args
        pos = start + jnp.arange(Q_BLOCK)[:, None] + offs[None, :]
        valid = (pos >= 0) & (pos < T)
        idx = jnp.clip(pos, 0, T - 1)
        k_sel = k[:, idx]
        v_sel = v[:, idx]
        s = jnp.einsum('bqhd,bqjhd->bhqj', qblk, k_sel,
                       preferred_element_type=jnp.float32) * scale
        s = jnp.where(valid[None, None], s, -jnp.inf)
        lse = jax.nn.logsumexp(s, axis=-1)
        p = jnp.exp(s - lse[..., None]).astype(v.dtype)
        return jnp.einsum('bhqj,bqjhd->bqhd', p, v_sel), lse

    o, lse = lax.map(block, (qb, starts))
    o = jnp.transpose(o, (1, 0, 2, 3, 4)).reshape(B, T, H, dh)
    lse = jnp.transpose(lse, (1, 0, 3, 2)).reshape(B, T, H)
    return o, lse


def _dilated_attention(q, k, v):
    B, T, G, H, dh = q.shape
    outs, lses = [], []
    for g, (window, dil) in enumerate(DIL_PATTERNS):
        o, lse = _dilated_group(q[:, :, g], k[:, :, g], v[:, :, g], window, dil)
        outs.append(o)
        lses.append(lse)
    w = jax.nn.softmax(jnp.stack(lses, axis=-1), axis=-1)
    o = jnp.einsum('btghd,bthg->bthd', jnp.stack(outs, axis=2).astype(jnp.float32), w)
    return o.astype(q.dtype).reshape(B, T, H * dh)


def _token_mixer(h, w_in, b_gate, rel_bias, lam_q1, lam_k1, lam_q2, lam_k2, diff_ln_g,
                 w_branch, w_out, layer_idx, cos, sin):
    B, T, _ = h.shape
    (f_in, nq, nk, nv, dq, dk, dv, lq, lk, lv, g_pre) = _split_cols(h @ w_in)
    heads = lambda t, n: t.reshape(B, T, n, HEAD_DIM)
    y_a = _fourier_mix(f_in.reshape(B, T, FNET_GROUPS, FNET_GROUP_DIM))
    y_b = _neighborhood_attention(heads(nq, NAT_HEADS), heads(nk, NAT_HEADS), heads(nv, NAT_HEADS), rel_bias)
    dq = _apply_rope(heads(dq, 2 * DIFF_HEADS), cos, sin).reshape(B, T, DIFF_HEADS, 2, HEAD_DIM)
    dk = _apply_rope(heads(dk, 2 * DIFF_HEADS), cos, sin).reshape(B, T, DIFF_HEADS, 2, HEAD_DIM)
    lam_init = 0.8 - 0.6 * math.exp(-0.3 * layer_idx)
    lam = (jnp.exp(jnp.sum(lam_q1.astype(jnp.float32) * lam_k1.astype(jnp.float32)))
           - jnp.exp(jnp.sum(lam_q2.astype(jnp.float32) * lam_k2.astype(jnp.float32))) + lam_init)
    y_c = _diff_attention(dq, dk, dv.reshape(B, T, DIFF_HEADS, DIFF_V_DIM), lam, lam_init, diff_ln_g)
    n_dil = DIL_GROUPS * DIL_HEADS
    lq = _apply_rope(heads(lq, n_dil), cos, sin).reshape(B, T, DIL_GROUPS, DIL_HEADS, HEAD_DIM)
    lk = _apply_rope(heads(lk, n_dil), cos, sin).reshape(B, T, DIL_GROUPS, DIL_HEADS, HEAD_DIM)
    y_d = _dilated_attention(lq, lk, lv.reshape(B, T, DIL_GROUPS, DIL_HEADS, HEAD_DIM))
    gates = jax.nn.sigmoid((g_pre + b_gate.reshape(-1)).astype(jnp.float32)).astype(h.dtype)
    gates = gates.reshape(B, T, N_BRANCH, D_MODEL)
    branches = [y_a, y_b, y_c, y_d]
    merged = gates[:, :, 0] * (branches[0] @ w_branch[0])
    for n in range(1, N_BRANCH):
        merged = merged + gates[:, :, n] * (branches[n] @ w_branch[n])
    return merged @ w_out


def _trunk(x, c, w_ada, b_ada, g_ffn1, w_ffn1_in, w_ffn1_out, g_mix, w_in, b_gate, nat_rel_bias,
           lam_q1, lam_k1, lam_q2, lam_k2, diff_ln_g, w_branch, w_out, g_ffn2, w_ffn2_in, w_ffn2_out, g_final):
    B, T, _ = x.shape
    cos, sin = _rope_tables(T)
    for l in range(DEPTH):
        mod = (jax.nn.silu(c) @ w_ada[l] + b_ada[l]).reshape(B, N_MOD, 1, D_MODEL)
        sh1, sc1, gt1, sh2, sc2, gt2, sh3, sc3, gt3 = [mod[:, i] for i in range(N_MOD)]
        h = _modulate(_rmsnorm(x, g_ffn1[l]), sh1, sc1)
        x = x + HALF * gt1 * _swiglu(h, w_ffn1_in[l], w_ffn1_out[l])
        h = _modulate(_rmsnorm(x, g_mix[l]), sh2, sc2)
        x = x + gt2 * _token_mixer(h, w_in[l], b_gate[l], nat_rel_bias[l], lam_q1[l], lam_k1[l],
                                   lam_q2[l], lam_k2[l], diff_ln_g[l], w_branch[l], w_out[l], l, cos, sin)
        h = _modulate(_rmsnorm(x, g_ffn2[l]), sh3, sc3)
        x = x + HALF * gt3 * _swiglu(h, w_ffn2_in[l], w_ffn2_out[l])
    return _rmsnorm(x, g_final)


def setup_inputs(seed: int = 0) -> dict:
    key = jax.random.key(seed)
    ks = jax.random.split(key, 28)
    f32 = jnp.float32
    L, D = DEPTH, D_MODEL

    def nrm(k, shape, scale):
        return jax.random.normal(k, shape, f32) * scale

    return {
        "x_prompt": nrm(ks[0], (BATCH, SEQ, D), 1.0),
        "x_sample": nrm(ks[1], (DEC_BATCH, DEC_SEQ, D), 1.0),
        "c_prompt": nrm(ks[2], (BATCH, D), 1.0),
        "c_sample": nrm(ks[3], (DEC_BATCH, D), 1.0),
        "w_ada": nrm(ks[4], (L, D, N_MOD * D), 0.5 * D ** -0.5),
        "b_ada": nrm(ks[5], (L, N_MOD * D), 0.02),
        "g_ffn1": 1.0 + nrm(ks[6], (L, D), 0.02),
        "w_ffn1_in": nrm(ks[7], (L, D, 2 * D_FF), D ** -0.5),
        "w_ffn1_out": nrm(ks[8], (L, D_FF, D), D_FF ** -0.5),
        "g_mix": 1.0 + nrm(ks[9], (L, D), 0.02),
        "w_in": nrm(ks[10], (L, D, IN_COLS), D ** -0.5),
        "b_gate": nrm(ks[11], (L, N_BRANCH, D), 0.02),
        "nat_rel_bias": nrm(ks[12], (L, NAT_HEADS, 2 * NAT_WIN_H - 1, 2 * NAT_WIN_W - 1), 0.1),
        "lam_q1": nrm(ks[13], (L, HEAD_DIM), 0.1),
        "lam_k1": nrm(ks[14], (L, HEAD_DIM), 0.1),
        "lam_q2": nrm(ks[15], (L, HEAD_DIM), 0.1),
        "lam_k2": nrm(ks[16], (L, HEAD_DIM), 0.1),
        "diff_ln_g": 1.0 + nrm(ks[17], (L, DIFF_V_DIM), 0.02),
        "w_branch": nrm(ks[18], (L, N_BRANCH, BRANCH_WIDTH, D), BRANCH_WIDTH ** -0.5),
        "w_out": nrm(ks[19], (L, D, D), D ** -0.5),
        "g_ffn2": 1.0 + nrm(ks[20], (L, D), 0.02),
        "w_ffn2_in": nrm(ks[21], (L, D, 2 * D_FF), D ** -0.5),
        "w_ffn2_out": nrm(ks[22], (L, D_FF, D), D_FF ** -0.5),
        "g_final": 1.0 + nrm(ks[23], (D,), 0.02),
    }


def reference(x_prompt, x_sample, c_prompt, c_sample, w_ada, b_ada, g_ffn1, w_ffn1_in, w_ffn1_out,
              g_mix, w_in, b_gate, nat_rel_bias, lam_q1, lam_k1, lam_q2, lam_k2, diff_ln_g,
              w_branch, w_out, g_ffn2, w_ffn2_in, w_ffn2_out, g_final):
    y_prompt = _trunk(x_prompt, c_prompt, w_ada, b_ada, g_ffn1, w_ffn1_in, w_ffn1_out, g_mix, w_in, b_gate,
                      nat_rel_bias, lam_q1, lam_k1, lam_q2, lam_k2, diff_ln_g, w_branch, w_out,
                      g_ffn2, w_ffn2_in, w_ffn2_out, g_final)
    y_sample = _trunk(x_sample, c_sample, w_ada, b_ada, g_ffn1, w_ffn1_in, w_ffn1_out, g_mix, w_in, b_gate,
                      nat_rel_bias, lam_q1, lam_k1, lam_q2, lam_k2, diff_ln_g, w_branch, w_out,
                      g_ffn2, w_ffn2_in, w_ffn2_out, g_final)
    return (y_prompt, y_sample)
```

```python
import functools
import math

import numpy as np
import jax
import jax.numpy as jnp
from jax import lax
from jax.experimental import pallas as pl
from jax.experimental.pallas import tpu as pltpu

F32 = jnp.float32
BF16 = jnp.bfloat16

D_MODEL = 2048
DEPTH = 2
GRID_W = 64
HEAD_DIM = 64
ROT_DIM = HEAD_DIM // 4
ROPE_THETA = 500000.0
EPS = 1e-6
HALF = 0.5
FNET_GROUP_DIM = 128
NAT_HEADS = 8
NAT_WIN_H = 8
NAT_WIN_W = 16
DIFF_HEADS = 4
DIL_DILATIONS = (1, 4, 16)
DIL_SIDE = 64
DIL_HEADS = 8
N_BRANCH = 4
BRANCH_WIDTH = 512
N_MOD = 9
D_FF = ((8 * D_MODEL // 3 + 63) // 64) * 64
MIX_COLS = 8192

LANES = 128
VMEM_LIMIT = 56 * 1024 * 1024
FF_CHUNK = 512
D_FF_PAD = ((D_FF + FF_CHUNK - 1) // FF_CHUNK) * FF_CHUNK
TOKEN_TILE = 512
PROJ_CHUNK = 1024
NEG_INF = -1e30


def _cparams(sem):
    return pltpu.CompilerParams(dimension_semantics=sem, vmem_limit_bytes=VMEM_LIMIT)


def _dot(a, b):
    return jnp.dot(a, b, preferred_element_type=F32)


def _dot_nt(a, b):
    return lax.dot_general(a, b, (((1,), (1,)), ((), ())), preferred_element_type=F32)


def _norm_modulate(x, g, sh, sc):
    ms = jnp.mean(x * x, axis=-1, keepdims=True)
    y = x * lax.rsqrt(ms + EPS) * g
    return y * (1.0 + sc) + sh


def _mod_kernel(c_ref, w_ref, b_ref, o_ref):
    c = c_ref[...]
    s = c * jax.nn.sigmoid(c)
    o_ref[0] = _dot(s.astype(BF16), w_ref[0].astype(BF16)) + b_ref[0]


def _modulation(c_all, w_ada, b_ada):
    L, D, N = w_ada.shape
    R = c_all.shape[0]
    tn = 1152
    return pl.pallas_call(
        _mod_kernel,
        grid=(L, N // tn),
        in_specs=[
            pl.BlockSpec((R, D), lambda l, j: (0, 0)),
            pl.BlockSpec((1, D, tn), lambda l, j: (l, 0, j)),
            pl.BlockSpec((1, 1, tn), lambda l, j: (l, 0, j)),
        ],
        out_specs=pl.BlockSpec((1, R, tn), lambda l, j: (l, 0, j)),
        out_shape=jax.ShapeDtypeStruct((L, R, N), F32),
        compiler_params=_cparams(("arbitrary", "arbitrary")),
    )(c_all, w_ada, b_ada.reshape(L, 1, N))


def _ffn_kernel(x_ref, g_ref, sh_ref, sc_ref, gt_ref, wab_ref, wo_ref, gf_ref, o_ref, h_scr, acc_scr,
                *, final_norm):
    k = pl.program_id(2)

    @pl.when(k == 0)
    def _():
        h = _norm_modulate(x_ref[0], g_ref[...], sh_ref[0], sc_ref[0])
        h_scr[...] = h.astype(BF16)
        acc_scr[...] = jnp.zeros_like(acc_scr)

    ab = _dot(h_scr[...], wab_ref[...])
    a = ab[:, :FF_CHUNK]
    b = ab[:, FF_CHUNK:]
    act = (a * jax.nn.sigmoid(a)) * b
    acc_scr[...] += _dot(act.astype(BF16), wo_ref[...])

    @pl.when(k == pl.num_programs(2) - 1)
    def _():
        xn = x_ref[0] + (HALF * gt_ref[0]) * acc_scr[...]
        if final_norm:
            ms = jnp.mean(xn * xn, axis=-1, keepdims=True)
            xn = xn * lax.rsqrt(ms + EPS) * gf_ref[...]
        o_ref[0] = xn


def _ffn(x, g, sh, sc, gt, wab, wo, g_final, final_norm):
    B, T, D = x.shape
    tm = min(TOKEN_TILE, T)
    nk = wo.shape[0] // FF_CHUNK
    vec = pl.BlockSpec((1, D), lambda b, i, k: (0, 0))
    per_b = pl.BlockSpec((1, 1, D), lambda b, i, k: (b, 0, 0))
    return pl.pallas_call(
        functools.partial(_ffn_kernel, final_norm=final_norm),
        grid=(B, T // tm, nk),
        in_specs=[
            pl.BlockSpec((1, tm, D), lambda b, i, k: (b, i, 0)),
            vec, per_b, per_b, per_b,
            pl.BlockSpec((D, 2 * FF_CHUNK), lambda b, i, k: (0, k)),
            pl.BlockSpec((FF_CHUNK, D), lambda b, i, k: (k, 0)),
            vec,
        ],
        out_specs=pl.BlockSpec((1, tm, D), lambda b, i, k: (b, i, 0)),
        out_shape=jax.ShapeDtypeStruct((B, T, D), F32),
        scratch_shapes=[pltpu.VMEM((tm, D), BF16), pltpu.VMEM((tm, D), F32)],
        compiler_params=_cparams(("arbitrary", "arbitrary", "arbitrary")),
    )(x, g, sh, sc, gt, wab, wo, g_final)


ROPE_STEPS = 4
REST_STEPS = 3
GATE_STEPS = N_BRANCH * D_MODEL // PROJ_CHUNK


def _inproj_kernel(x_ref, g_ref, sh_ref, sc_ref, w_ref, bg_ref, ra_ref, rb_ref, rc_ref,
                   rope_o, f_o, nq_o, rest_o, gate_o, h_scr):
    j = pl.program_id(2)

    @pl.when(j == 0)
    def _():
        h = _norm_modulate(x_ref[0], g_ref[...], sh_ref[0], sc_ref[0])
        h_scr[...] = h.astype(BF16)

    z = _dot(h_scr[...], w_ref[...])

    @pl.when(j < ROPE_STEPS)
    def _():
        ra = ra_ref[...]
        rb = rb_ref[...]
        rc = rc_ref[...]
        for c in range(PROJ_CHUNK // LANES):
            zc = z[:, c * LANES:(c + 1) * LANES]
            r = zc * ra + pltpu.roll(zc, ROT_DIM // 2, 1) * rb + pltpu.roll(zc, LANES - ROT_DIM // 2, 1) * rc
            rope_o[0, :, c * LANES:(c + 1) * LANES] = r.astype(BF16)

    @pl.when(j == ROPE_STEPS)
    def _():
        f_o[0] = z[:, :BRANCH_WIDTH].astype(BF16)
        nq_o[0] = z[:, BRANCH_WIDTH:].astype(BF16)

    @pl.when((j > ROPE_STEPS) & (j <= ROPE_STEPS + REST_STEPS))
    def _():
        rest_o[0] = z.astype(BF16)

    @pl.when(j > ROPE_STEPS + REST_STEPS)
    def _():
        gate_o[0] = jax.nn.sigmoid(z + bg_ref[...]).astype(BF16)


def _inproj(x, g, sh, sc, w, bg, ra, rb, rc):
    B, T, D = x.shape
    tm = min(TOKEN_TILE, T)
    n_steps = w.shape[1] // PROJ_CHUNK
    first_gate = ROPE_STEPS + 1 + REST_STEPS
    vec = pl.BlockSpec((1, D), lambda b, i, j: (0, 0))
    per_b = pl.BlockSpec((1, 1, D), lambda b, i, j: (b, 0, 0))
    tab = pl.BlockSpec((tm, LANES), lambda b, i, j: (i, 0))
    return pl.pallas_call(
        _inproj_kernel,
        grid=(B, T // tm, n_steps),
        in_specs=[
            pl.BlockSpec((1, tm, D), lambda b, i, j: (b, i, 0)),
            vec, per_b, per_b,
            pl.BlockSpec((D, PROJ_CHUNK), lambda b, i, j: (0, j)),
            pl.BlockSpec((1, PROJ_CHUNK), lambda b, i, j: (0, jnp.maximum(j - first_gate, 0))),
            tab, tab, tab,
        ],
        out_specs=[
            pl.BlockSpec((1, tm, PROJ_CHUNK), lambda b, i, j: (b, i, jnp.minimum(j, ROPE_STEPS - 1))),
            pl.BlockSpec((1, tm, BRANCH_WIDTH), lambda b, i, j: (b, i, 0)),
            pl.BlockSpec((1, tm, BRANCH_WIDTH), lambda b, i, j: (b, i, 0)),
            pl.BlockSpec((1, tm, PROJ_CHUNK),
                         lambda b, i, j: (b, i, jnp.clip(j - ROPE_STEPS - 1, 0, REST_STEPS - 1))),
            pl.BlockSpec((1, tm, PROJ_CHUNK), lambda b, i, j: (b, i, jnp.maximum(j - first_gate, 0))),
        ],
        out_shape=[
            jax.ShapeDtypeStruct((B, T, ROPE_STEPS * PROJ_CHUNK), BF16),
            jax.ShapeDtypeStruct((B, T, BRANCH_WIDTH), BF16),
            jax.ShapeDtypeStruct((B, T, BRANCH_WIDTH), BF16),
            jax.ShapeDtypeStruct((B, T, REST_STEPS * PROJ_CHUNK), BF16),
            jax.ShapeDtypeStruct((B, T, GATE_STEPS * PROJ_CHUNK), BF16),
        ],
        scratch_shapes=[pltpu.VMEM((tm, D), BF16)],
        compiler_params=_cparams(("arbitrary", "arbitrary", "arbitrary")),
    )(x, g, sh, sc, w, bg, ra, rb, rc)


FFT_T2 = 128
FFT_COLS = 2048
FFT_K1_BLOCK = 4


def _fft1_kernel(u_ref, c1_ref, s1_ref, ar_ref, ai_ref):
    u = u_ref[0]
    ar_ref[0] = _dot(c1_ref[...], u)
    ai_ref[0] = -_dot(s1_ref[...], u)


def _fft2_kernel(ar_ref, ai_ref, twc_ref, tws_ref, c2_ref, s2_ref, cc_ref, sc_ref, o_ref, *, norm):
    c2 = c2_ref[...]
    s2 = s2_ref[...]
    cc = cc_ref[...]
    sc = sc_ref[...]
    for kk in range(FFT_K1_BLOCK):
        rows = slice(kk * FFT_T2, (kk + 1) * FFT_T2)
        ar = ar_ref[0, rows, :]
        ai = ai_ref[0, rows, :]
        twc = jnp.concatenate([twc_ref[rows, :]] * (BRANCH_WIDTH // LANES), axis=1)
        tws = jnp.concatenate([tws_ref[rows, :]] * (BRANCH_WIDTH // LANES), axis=1)
        br = (ar * twc + ai * tws).astype(BF16)
        bi = (ai * twc - ar * tws).astype(BF16)
        zr = _dot(c2, br) + _dot(s2, bi)
        zi = _dot(c2, bi) - _dot(s2, br)
        y = _dot(zr.astype(BF16), cc) + _dot(zi.astype(BF16), sc)
        o_ref[0, :, kk * BRANCH_WIDTH:(kk + 1) * BRANCH_WIDTH] = (y * norm).astype(BF16)


def _dft_tables(T):
    T1 = T // FFT_T2
    k1 = np.arange(T1)
    a1 = 2.0 * np.pi * np.outer(k1, k1) / T1
    k2 = np.arange(FFT_T2)
    a2 = 2.0 * np.pi * np.outer(k2, k2) / FFT_T2
    atw = 2.0 * np.pi * np.outer(k1, k2).reshape(T, 1) / T
    atw = np.broadcast_to(atw, (T, LANES))
    ch = np.arange(FNET_GROUP_DIM)
    ac = 2.0 * np.pi * np.outer(ch, ch) / FNET_GROUP_DIM
    eye = np.eye(BRANCH_WIDTH // FNET_GROUP_DIM)
    bf = lambda a: jnp.asarray(a, dtype=BF16)
    return dict(c1=bf(np.cos(a1)), s1=bf(np.sin(a1)), c2=bf(np.cos(a2)), s2=bf(np.sin(a2)),
                twc=jnp.asarray(np.cos(atw), F32), tws=jnp.asarray(np.sin(atw), F32),
                cc=bf(np.kron(eye, np.cos(ac))), sc=bf(np.kron(eye, np.sin(ac))))


def _fourier_mix(f, tabs):
    B, T, C = f.shape
    T1 = T // FFT_T2
    n_col = FFT_T2 * C // FFT_COLS
    full2 = lambda shape: pl.BlockSpec(shape, lambda b, i: (0, 0))
    ar, ai = pl.pallas_call(
        _fft1_kernel,
        grid=(B, n_col),
        in_specs=[pl.BlockSpec((1, T1, FFT_COLS), lambda b, i: (b, 0, i)),
                  full2((T1, T1)), full2((T1, T1))],
        out_specs=[pl.BlockSpec((1, T1, FFT_COLS), lambda b, i: (b, 0, i))] * 2,
        out_shape=[jax.ShapeDtypeStruct((B, T1, FFT_T2 * C), F32)] * 2,
        compiler_params=_cparams(("arbitrary", "arbitrary")),
    )(f.reshape(B, T1, FFT_T2 * C), tabs["c1"], tabs["s1"])
    ar = ar.reshape(B, T, C)
    ai = ai.reshape(B, T, C)
    rows = FFT_K1_BLOCK * FFT_T2
    y = pl.pallas_call(
        functools.partial(_fft2_kernel, norm=1.0 / math.sqrt(T * FNET_GROUP_DIM)),
        grid=(B, T1 // FFT_K1_BLOCK),
        in_specs=[pl.BlockSpec((1, rows, C), lambda b, i: (b, i, 0)),
                  pl.BlockSpec((1, rows, C), lambda b, i: (b, i, 0)),
                  pl.BlockSpec((rows, LANES), lambda b, i: (i, 0)),
                  pl.BlockSpec((rows, LANES), lambda b, i: (i, 0)),
                  full2((FFT_T2, FFT_T2)), full2((FFT_T2, FFT_T2)),
                  full2((C, C)), full2((C, C))],
        out_specs=pl.BlockSpec((1, FFT_T2, FFT_K1_BLOCK * C), lambda b, i: (b, 0, i)),
        out_shape=jax.ShapeDtypeStruct((B, FFT_T2, T1 * C), BF16),
        compiler_params=_cparams(("arbitrary", "arbitrary")),
    )(ar, ai, tabs["twc"], tabs["tws"], tabs["c2"], tabs["s2"], tabs["cc"], tabs["sc"])
    return y.reshape(B, T, C)


NAT_Q_ROWS = 8
NAT_Q_TOK = NAT_Q_ROWS * GRID_W
NAT_EDGE_TOK = (NAT_WIN_H // 2) * GRID_W
NAT_WIN_TOK = NAT_Q_TOK + 2 * NAT_EDGE_TOK
NAT_SLAB_TOK = NAT_WIN_H * GRID_W


def _head_masks():
    lane = lax.broadcasted_iota(jnp.int32, (1, LANES), 1)
    return lane < HEAD_DIM


def _nat_kernel(q_ref, kp_ref, kc_ref, kn_ref, vp_ref, vc_ref, vn_ref, bias_ref, o_ref, kw, vw, *, rows):
    i = pl.program_id(1)
    kw[0:NAT_EDGE_TOK, :] = kp_ref[0]
    kw[NAT_EDGE_TOK:NAT_EDGE_TOK + NAT_Q_TOK, :] = kc_ref[0]
    kw[NAT_EDGE_TOK + NAT_Q_TOK:, :] = kn_ref[0]
    vw[0:NAT_EDGE_TOK, :] = vp_ref[0]
    vw[NAT_EDGE_TOK:NAT_EDGE_TOK + NAT_Q_TOK, :] = vc_ref[0]
    vw[NAT_EDGE_TOK + NAT_Q_TOK:, :] = vn_ref[0]
    first = _head_masks()
    r0 = i * NAT_Q_ROWS

    def row_body(rr, carry):
        r = r0 + rr
        rs = jnp.clip(r - NAT_WIN_H // 2, 0, rows - NAT_WIN_H)
        var = r - rs
        w0 = pl.multiple_of((rs - (r0 - NAT_WIN_H // 2)) * GRID_W, GRID_W)
        q0 = pl.multiple_of(rr * GRID_W, GRID_W)
        for hp in range(NAT_HEADS // 2):
            cols = slice(hp * LANES, (hp + 1) * LANES)
            q = q_ref[0, pl.ds(q0, GRID_W), cols]
            k = kw[pl.ds(w0, NAT_SLAB_TOK), cols]
            v = vw[pl.ds(w0, NAT_SLAB_TOK), cols]
            outs = []
            for hh in range(2):
                sel = first if hh == 0 else jnp.logical_not(first)
                qm = jnp.where(sel, q, jnp.zeros_like(q))
                s = _dot_nt(qm, k) + bias_ref[var, 2 * hp + hh]
                m = jnp.max(s, axis=-1, keepdims=True)
                p = jnp.exp(s - m)
                l = jnp.sum(p, axis=-1, keepdims=True)
                outs.append(_dot(p.astype(BF16), v) / l)
            o = jnp.where(first, outs[0], outs[1])
            o_ref[0, pl.ds(q0, GRID_W), cols] = o.astype(BF16)
        return carry

    lax.fori_loop(0, NAT_Q_ROWS, row_body, 0)


def _nat_bias_table(rel_bias):
    H = rel_bias.shape[0]
    col = np.arange(GRID_W)
    col_start = np.clip(col - NAT_WIN_W // 2, 0, GRID_W - NAT_WIN_W)
    kc = np.arange(GRID_W)
    valid = (kc[None, :] >= col_start[:, None]) & (kc[None, :] < col_start[:, None] + NAT_WIN_W)
    col_off = np.clip(kc[None, :] - col[:, None] + (NAT_WIN_W - 1), 0, 2 * NAT_WIN_W - 2)
    d = np.arange(NAT_WIN_H)
    a = np.arange(NAT_WIN_H)
    row_off = a[None, :] - d[:, None] + (NAT_WIN_H - 1)
    tab = rel_bias[:, row_off][:, :, :, col_off]
    tab = jnp.where(jnp.asarray(valid)[None, None, None], tab.astype(F32), NEG_INF)
    tab = jnp.transpose(tab, (1, 0, 3, 2, 4))
    return tab.reshape(NAT_WIN_H, H, GRID_W, NAT_SLAB_TOK)


def _neighborhood_attention(nq, rest, bias_tab):
    B, T, C = nq.shape
    rows = T // GRID_W
    n_blk = T // NAT_Q_TOK
    per = NAT_Q_TOK // NAT_EDGE_TOK
    n_edge = T // NAT_EDGE_TOK
    prev = lambda c: (lambda b, i: (b, jnp.maximum(i * per - 1, 0), c))
    cur = lambda c: (lambda b, i: (b, i, c))
    nxt = lambda c: (lambda b, i: (b, jnp.minimum((i + 1) * per, n_edge - 1), c))
    edge = lambda f: pl.BlockSpec((1, NAT_EDGE_TOK, C), f)
    mid = lambda f: pl.BlockSpec((1, NAT_Q_TOK, C), f)
    return pl.pallas_call(
        functools.partial(_nat_kernel, rows=rows),
        grid=(B, n_blk),
        in_specs=[mid(cur(0)),
                  edge(prev(0)), mid(cur(0)), edge(nxt(0)),
                  edge(prev(1)), mid(cur(1)), edge(nxt(1)),
                  pl.BlockSpec(bias_tab.shape, lambda b, i: (0, 0, 0, 0))],
        out_specs=mid(cur(0)),
        out_shape=jax.ShapeDtypeStruct((B, T, C), BF16),
        scratch_shapes=[pltpu.VMEM((NAT_WIN_TOK, C), BF16), pltpu.VMEM((NAT_WIN_TOK, C), BF16)],
        compiler_params=_cparams(("arbitrary", "arbitrary")),
    )(nq, rest, rest, rest, rest, rest, rest, bias_tab)


DIFF_TQ = 512
DIFF_TK = 512


def _diff_kernel(q_ref, k_ref, v_ref, lq1_ref, lk1_ref, lq2_ref, lk2_ref, g_ref, o_ref, *, lam_init, n_kv):
    first = _head_masks()
    q = q_ref[0]
    q1 = jnp.where(first, q, jnp.zeros_like(q))
    q2 = jnp.where(first, jnp.zeros_like(q), q)
    tq = q.shape[0]

    def body(c, carry):
        m1, l1, a1, m2, l2, a2 = carry
        k0 = pl.multiple_of(c * DIFF_TK, DIFF_TK)
        k = k_ref[0, pl.ds(k0, DIFF_TK), :]
        v = v_ref[0, pl.ds(k0, DIFF_TK), :]

        def one(qm, m, l, a):
            s = _dot_nt(qm, k)
            mn = jnp.maximum(m, jnp.max(s, axis=-1, keepdims=True))
            al = jnp.exp(m - mn)
            p = jnp.exp(s - mn)
            l = al * l + jnp.sum(p, axis=-1, keepdims=True)
            a = al * a + _dot(p.astype(BF16), v)
            return mn, l, a

        m1, l1, a1 = one(q1, m1, l1, a1)
        m2, l2, a2 = one(q2, m2, l2, a2)
        return m1, l1, a1, m2, l2, a2

    neg = jnp.full((tq, 1), NEG_INF, F32)
    zero1 = jnp.zeros((tq, 1), F32)
    zacc = jnp.zeros((tq, LANES), F32)
    m1, l1, a1, m2, l2, a2 = lax.fori_loop(0, n_kv, body, (neg, zero1, zacc, neg, zero1, zacc))
    lam = (jnp.exp(jnp.sum(lq1_ref[...] * lk1_ref[...], keepdims=True))
           - jnp.exp(jnp.sum(lq2_ref[...] * lk2_ref[...], keepdims=True)) + lam_init)
    o = a1 / l1 - lam * (a2 / l2)
    ms = jnp.mean(o * o, axis=-1, keepdims=True)
    o = o * lax.rsqrt(ms + EPS) * g_ref[...]
    o_ref[0] = (o * (1.0 - lam_init)).astype(BF16)


def _diff_attention(rope, rest, lq1, lk1, lq2, lk2, ln_g, lam_init):
    B, T, _ = rope.shape
    tq = min(DIFF_TQ, T)
    k_blk = 2048 // LANES
    v_blk = 1024 // LANES
    vec = lambda n: pl.BlockSpec((1, n), lambda b, h, i: (0, 0))
    return pl.pallas_call(
        functools.partial(_diff_kernel, lam_init=lam_init, n_kv=T // DIFF_TK),
        grid=(B, DIFF_HEADS, T // tq),
        in_specs=[pl.BlockSpec((1, tq, LANES), lambda b, h, i: (b, i, h)),
                  pl.BlockSpec((1, T, LANES), lambda b, h, i: (b, 0, k_blk + h)),
                  pl.BlockSpec((1, T, LANES), lambda b, h, i: (b, 0, v_blk + h)),
                  vec(HEAD_DIM), vec(HEAD_DIM), vec(HEAD_DIM), vec(HEAD_DIM), vec(LANES)],
        out_specs=pl.BlockSpec((1, tq, LANES), lambda b, h, i: (b, i, h)),
        out_shape=jax.ShapeDtypeStruct((B, T, DIFF_HEADS * LANES), BF16),
        compiler_params=_cparams(("arbitrary", "arbitrary", "arbitrary")),
    )(rope, rope, rest, lq1, lk1, lq2, lk2, ln_g)


DIL_TQ = 512
DIL_EDGE = 128


def _dil_kernel(q_ref, kp_ref, kc_ref, kn_ref, vp_ref, vc_ref, vn_ref, o_ref, lse_ref, kw, vw, *, seq, tq):
    i = pl.program_id(2)
    kw[0:DIL_EDGE, :] = kp_ref[0]
    kw[DIL_EDGE:DIL_EDGE + tq, :] = kc_ref[0]
    kw[DIL_EDGE + tq:, :] = kn_ref[0]
    vw[0:DIL_EDGE, :] = vp_ref[0]
    vw[DIL_EDGE:DIL_EDGE + tq, :] = vc_ref[0]
    vw[DIL_EDGE + tq:, :] = vn_ref[0]
    win = tq + 2 * DIL_EDGE
    s0 = i * tq
    qpos = s0 + lax.broadcasted_iota(jnp.int32, (tq, win), 0)
    kpos = s0 - DIL_EDGE + lax.broadcasted_iota(jnp.int32, (tq, win), 1)
    valid = (jnp.abs(kpos - qpos) <= DIL_SIDE) & (kpos >= 0) & (kpos < seq)
    first = _head_masks()
    for hp in range(DIL_HEADS // 2):
        cols = slice(hp * LANES, (hp + 1) * LANES)
        q = q_ref[0, :, cols]
        k = kw[:, cols]
        v = vw[:, cols]
        outs, lses = [], []
        for hh in range(2):
            sel = first if hh == 0 else jnp.logical_not(first)
            qm = jnp.where(sel, q, jnp.zeros_like(q))
            s = jnp.where(valid, _dot_nt(qm, k), NEG_INF)
            m = jnp.max(s, axis=-1, keepdims=True)
            p = jnp.exp(s - m)
            l = jnp.sum(p, axis=-1, keepdims=True)
            outs.append(_dot(p.astype(BF16), v) / l)
            lses.append(m + jnp.log(l))
        o_ref[0, :, cols] = jnp.where(first, outs[0], outs[1]).astype(BF16)
        lse_ref[0, :, cols] = jnp.where(first, lses[0], lses[1])


def _dilated_group(rope, rest, group, dil):
    B, T, CR = rope.shape
    CV = rest.shape[2]
    C = DIL_HEADS * HEAD_DIM
    seq = T // dil
    tq = min(DIL_TQ, seq)
    per = tq // DIL_EDGE
    n_edge = seq // DIL_EDGE
    ropev = rope.reshape(B, seq, dil * CR)
    restv = rest.reshape(B, seq, dil * CV)
    qc = lambda r: r * (CR // C) + 1 + group
    kc = lambda r: r * (CR // C) + 5 + group
    vc = lambda r: r * (CV // C) + 3 + group
    prev = lambda cf: (lambda b, r, i: (b, jnp.maximum(i * per - 1, 0), cf(r)))
    cur = lambda cf: (lambda b, r, i: (b, i, cf(r)))
    nxt = lambda cf: (lambda b, r, i: (b, jnp.minimum((i + 1) * per, n_edge - 1), cf(r)))
    edge = lambda f: pl.BlockSpec((1, DIL_EDGE, C), f)
    mid = lambda f: pl.BlockSpec((1, tq, C), f)
    out_map = lambda b, r, i: (b, i, r)
    o, lse = pl.pallas_call(
        functools.partial(_dil_kernel, seq=seq, tq=tq),
        grid=(B, dil, seq // tq),
        in_specs=[mid(cur(qc)),
                  edge(prev(kc)), mid(cur(kc)), edge(nxt(kc)),
                  edge(prev(vc)), mid(cur(vc)), edge(nxt(vc))],
        out_specs=[pl.BlockSpec((1, tq, C), out_map), pl.BlockSpec((1, tq, C), out_map)],
        out_shape=[jax.ShapeDtypeStruct((B, seq, dil * C), BF16),
                   jax.ShapeDtypeStruct((B, seq, dil * C), F32)],
        scratch_shapes=[pltpu.VMEM((tq + 2 * DIL_EDGE, C), BF16), pltpu.VMEM((tq + 2 * DIL_EDGE, C), BF16)],
        compiler_params=_cparams(("arbitrary", "arbitrary", "arbitrary")),
    )(ropev, ropev, ropev, ropev, restv, restv, restv)
    return o.reshape(B, T, C), lse.reshape(B, T, C)


MERGE_CHUNK = 512


def _merge_kernel(x_ref, gt_ref, ya_ref, yb_ref, yc_ref, o0_ref, o1_ref, o2_ref, s0_ref, s1_ref, s2_ref,
                  g0_ref, g1_ref, g2_ref, g3_ref, wb_ref, wo_ref, out_ref, yd_scr, acc_scr):
    j = pl.program_id(2)

    @pl.when(j == 0)
    def _():
        s0 = s0_ref[0]
        s1 = s1_ref[0]
        s2 = s2_ref[0]
        m = jnp.maximum(jnp.maximum(s0, s1), s2)
        e0 = jnp.exp(s0 - m)
        e1 = jnp.exp(s1 - m)
        e2 = jnp.exp(s2 - m)
        num = (o0_ref[0].astype(F32) * e0 + o1_ref[0].astype(F32) * e1 + o2_ref[0].astype(F32) * e2)
        yd_scr[...] = (num / (e0 + e1 + e2)).astype(BF16)
        acc_scr[...] = jnp.zeros_like(acc_scr)

    merged = g0_ref[0].astype(F32) * _dot(ya_ref[0], wb_ref[0])
    merged += g1_ref[0].astype(F32) * _dot(yb_ref[0], wb_ref[1])
    merged += g2_ref[0].astype(F32) * _dot(yc_ref[0], wb_ref[2])
    merged += g3_ref[0].astype(F32) * _dot(yd_scr[...], wb_ref[3])
    acc_scr[...] += _dot(merged.astype(BF16), wo_ref[...])

    @pl.when(j == pl.num_programs(2) - 1)
    def _():
        out_ref[0] = x_ref[0] + gt_ref[0] * acc_scr[...]


def _merge(x, gt, ya, yb, yc, dil_o, dil_lse, gates, wb, wo):
    B, T, D = x.shape
    tm = min(TOKEN_TILE, T)
    C = BRANCH_WIDTH
    n_j = D // MERGE_CHUNK
    tok = pl.BlockSpec((1, tm, C), lambda b, i, j: (b, i, 0))
    gate = lambda n: pl.BlockSpec((1, tm, MERGE_CHUNK), lambda b, i, j: (b, i, n * n_j + j))
    return pl.pallas_call(
        _merge_kernel,
        grid=(B, T // tm, n_j),
        in_specs=[pl.BlockSpec((1, tm, D), lambda b, i, j: (b, i, 0)),
                  pl.BlockSpec((1, 1, D), lambda b, i, j: (b, 0, 0)),
                  tok, tok, tok, tok, tok, tok, tok, tok, tok,
                  gate(0), gate(1), gate(2), gate(3),
                  pl.BlockSpec((N_BRANCH, C, MERGE_CHUNK), lambda b, i, j: (0, 0, j)),
                  pl.BlockSpec((MERGE_CHUNK, D), lambda b, i, j: (j, 0))],
        out_specs=pl.BlockSpec((1, tm, D), lambda b, i, j: (b, i, 0)),
        out_shape=jax.ShapeDtypeStruct((B, T, D), F32),
        scratch_shapes=[pltpu.VMEM((tm, C), BF16), pltpu.VMEM((tm, D), F32)],
        compiler_params=_cparams(("arbitrary", "arbitrary", "arbitrary")),
    )(x, gt, ya, yb, yc, *dil_o, *dil_lse, gates, gates, gates, gates, wb, wo)


def _prep_ffn(w_in, w_out):
    D = w_in.shape[0]
    pad = D_FF_PAD - D_FF
    a = jnp.pad(w_in[:, :D_FF].astype(BF16), ((0, 0), (0, pad)))
    b = jnp.pad(w_in[:, D_FF:].astype(BF16), ((0, 0), (0, pad)))
    n = D_FF_PAD // FF_CHUNK
    wab = jnp.concatenate([a.reshape(D, n, FF_CHUNK), b.reshape(D, n, FF_CHUNK)], axis=2)
    wo = jnp.pad(w_out.astype(BF16), ((0, pad), (0, 0)))
    return wab.reshape(D, n * 2 * FF_CHUNK), wo


def _prep_w_in(w):
    W = BRANCH_WIDTH
    scale = HEAD_DIM ** -0.5
    f_in, nq, nk, nv, dq, dk, dv = [w[:, n * W:(n + 1) * W] for n in range(7)]
    lq = w[:, 7 * W:10 * W]
    lk = w[:, 10 * W:13 * W]
    lv = w[:, 13 * W:16 * W]
    gates = w[:, MIX_COLS:]
    cols = [dq * scale, lq * scale, dk, lk, f_in, nq * scale, nk, nv, dv, lv, gates]
    return jnp.concatenate(cols, axis=1).astype(BF16)


def _rope_tables(T):
    half = ROT_DIM // 2
    inv = ROPE_THETA ** (-jnp.arange(0, ROT_DIM, 2, dtype=F32) / ROT_DIM)
    ang = jnp.arange(T, dtype=F32)[:, None] * inv[None, :]
    cos, sin = jnp.cos(ang), jnp.sin(ang)
    pad = HEAD_DIM - ROT_DIM
    ra = jnp.concatenate([cos, cos, jnp.ones((T, pad), F32)], axis=1)
    rb = jnp.concatenate([jnp.zeros((T, half), F32), sin, jnp.zeros((T, pad), F32)], axis=1)
    rc = jnp.concatenate([-sin, jnp.zeros((T, half + pad), F32)], axis=1)
    rep = LANES // HEAD_DIM
    return tuple(jnp.tile(t, (1, rep)) for t in (ra, rb, rc))


def _trunk(x, mods, P, tabs):
    B, T, D = x.shape
    for l in range(DEPTH):
        sh1, sc1, gt1, sh2, sc2, gt2, sh3, sc3, gt3 = [mods[l][:, i:i + 1, :] for i in range(N_MOD)]
        x = _ffn(x, P["g_ffn1"][l], sh1, sc1, gt1, P["wab1"][l], P["wo1"][l], P["g_final"], False)
        rope, f_in, nq, rest, gates = _inproj(x, P["g_mix"][l], sh2, sc2, P["w_in"][l], P["b_gate"][l],
                                              *tabs["rope"])
        ya = _fourier_mix(f_in, tabs["dft"])
        yb = _neighborhood_attention(nq, rest, P["nat_bias"][l])
        lam_init = 0.8 - 0.6 * math.exp(-0.3 * l)
        yc = _diff_attention(rope, rest, P["lam_q1"][l], P["lam_k1"][l], P["lam_q2"][l], P["lam_k2"][l],
                             P["diff_ln_g"][l], lam_init)
        dil = [_dilated_group(rope, rest, g, d) for g, d in enumerate(DIL_DILATIONS)]
        x = _merge(x, gt2, ya, yb, yc, [o for o, _ in dil], [s for _, s in dil], gates,
                   P["w_branch"][l], P["w_out"][l])
        x = _ffn(x, P["g_ffn2"][l], sh3, sc3, gt3, P["wab2"][l], P["wo2"][l], P["g_final"], l == DEPTH - 1)
    return x


def kernel(x_prompt, x_sample, c_prompt, c_sample, w_ada, b_ada, g_ffn1, w_ffn1_in, w_ffn1_out, g_mix, w_in, b_gate, nat_rel_bias, lam_q1, lam_k1, lam_q2, lam_k2, diff_ln_g, w_branch, w_out, g_ffn2, w_ffn2_in, w_ffn2_out, g_final):
    L = DEPTH
    D = D_MODEL
    Bp, Bs = c_prompt.shape[0], c_sample.shape[0]
    rows = ((Bp + Bs + 7) // 8) * 8
    c_all = jnp.concatenate([c_prompt, c_sample, jnp.zeros((rows - Bp - Bs, D), F32)], axis=0)
    mods = _modulation(c_all, w_ada, b_ada).reshape(L, rows, N_MOD, D)

    P = {}
    ffn1 = [_prep_ffn(w_ffn1_in[l], w_ffn1_out[l]) for l in range(L)]
    ffn2 = [_prep_ffn(w_ffn2_in[l], w_ffn2_out[l]) for l in range(L)]
    P["wab1"] = [a for a, _ in ffn1]
    P["wo1"] = [b for _, b in ffn1]
    P["wab2"] = [a for a, _ in ffn2]
    P["wo2"] = [b for _, b in ffn2]
    P["w_in"] = [_prep_w_in(w_in[l]) for l in range(L)]
    P["b_gate"] = [b_gate[l].reshape(1, N_BRANCH * D) for l in range(L)]
    P["nat_bias"] = [_nat_bias_table(nat_rel_bias[l]) for l in range(L)]
    P["w_branch"] = [w_branch[l].astype(BF16) for l in range(L)]
    P["w_out"] = [w_out[l].astype(BF16) for l in range(L)]
    for name, arr in (("g_ffn1", g_ffn1), ("g_mix", g_mix), ("g_ffn2", g_ffn2), ("lam_q1", lam_q1),
                      ("lam_k1", lam_k1), ("lam_q2", lam_q2), ("lam_k2", lam_k2), ("diff_ln_g", diff_ln_g)):
        P[name] = [arr[l].reshape(1, -1) for l in range(L)]
    P["g_final"] = g_final.reshape(1, D)

    outs = []
    for x, lo, hi in ((x_prompt, 0, Bp), (x_sample, Bp, Bp + Bs)):
        T = x.shape[1]
        tabs = {"rope": _rope_tables(T), "dft": _dft_tables(T)}
        outs.append(_trunk(x, mods[:, lo:hi], P, tabs))
    return tuple(outs)
```

```python
import functools
import math

import numpy as np
import jax
import jax.numpy as jnp
from jax import lax
from jax.experimental import pallas as pl
from jax.experimental.pallas import tpu as pltpu

F32 = jnp.float32
BF16 = jnp.bfloat16

D_MODEL = 2048
DEPTH = 2
GRID_W = 64
HEAD_DIM = 64
ROT_DIM = HEAD_DIM // 4
ROPE_THETA = 500000.0
EPS = 1e-6
HALF = 0.5
FNET_GROUP_DIM = 128
NAT_HEADS = 8
NAT_WIN_H = 8
NAT_WIN_W = 16
DIFF_HEADS = 4
DIL_DILATIONS = (1, 4, 16)
DIL_SIDE = 64
DIL_HEADS = 8
N_BRANCH = 4
BRANCH_WIDTH = 512
N_MOD = 9
D_FF = ((8 * D_MODEL // 3 + 63) // 64) * 64
MIX_COLS = 8192

LANES = 128
VMEM_LIMIT = 56 * 1024 * 1024
FF_CHUNK = 512
D_FF_PAD = ((D_FF + FF_CHUNK - 1) // FF_CHUNK) * FF_CHUNK
TOKEN_TILE = 512
PROJ_CHUNK = 1024
MXU_COLS = 256
NEG_INF = -1e30
LOG2E = math.log2(math.e)


def _cparams(sem):
    return pltpu.CompilerParams(dimension_semantics=sem, vmem_limit_bytes=VMEM_LIMIT)


def _dot(a, b):
    return jnp.dot(a, b, preferred_element_type=F32)


def _dot_nt(a, b):
    return lax.dot_general(a, b, (((1,), (1,)), ((), ())), preferred_element_type=F32)


def _norm_modulate(x, g, sh, sc):
    ms = jnp.mean(x * x, axis=-1, keepdims=True)
    y = x * lax.rsqrt(ms + EPS) * g
    return y * (1.0 + sc) + sh


def _mod_kernel(c_ref, w_ref, b_ref, o_ref):
    c = c_ref[...]
    s = c * jax.nn.sigmoid(c)
    o_ref[0] = _dot(s.astype(BF16), w_ref[0].astype(BF16)) + b_ref[0]


def _modulation(c_all, w_ada, b_ada):
    L, D, N = w_ada.shape
    R = c_all.shape[0]
    tn = 1152
    return pl.pallas_call(
        _mod_kernel,
        grid=(L, N // tn),
        in_specs=[
            pl.BlockSpec((R, D), lambda l, j: (0, 0)),
            pl.BlockSpec((1, D, tn), lambda l, j: (l, 0, j)),
            pl.BlockSpec((1, 1, tn), lambda l, j: (l, 0, j)),
        ],
        out_specs=pl.BlockSpec((1, R, tn), lambda l, j: (l, 0, j)),
        out_shape=jax.ShapeDtypeStruct((L, R, N), F32),
        compiler_params=_cparams(("arbitrary", "arbitrary")),
    )(c_all, w_ada, b_ada.reshape(L, 1, N))


def _ffn_kernel(x_ref, g_ref, sh_ref, sc_ref, gt_ref, wab_ref, wo_ref, gf_ref, o_ref, h_scr, acc_scr,
                *, final_norm):
    k = pl.program_id(2)

    @pl.when(k == 0)
    def _():
        h = _norm_modulate(x_ref[0], g_ref[...], sh_ref[0], sc_ref[0])
        h_scr[...] = h.astype(BF16)
        acc_scr[...] = jnp.zeros_like(acc_scr)

    ab = _dot(h_scr[...], wab_ref[...])
    a = ab[:, :FF_CHUNK]
    b = ab[:, FF_CHUNK:]
    act = (a * jax.nn.sigmoid(a)) * b
    acc_scr[...] += _dot(act.astype(BF16), wo_ref[...])

    @pl.when(k == pl.num_programs(2) - 1)
    def _():
        xn = x_ref[0] + (HALF * gt_ref[0]) * acc_scr[...]
        if final_norm:
            ms = jnp.mean(xn * xn, axis=-1, keepdims=True)
            xn = xn * lax.rsqrt(ms + EPS) * gf_ref[...]
        o_ref[0] = xn


def _ffn(x, g, sh, sc, gt, wab, wo, g_final, final_norm):
    B, T, D = x.shape
    tm = min(TOKEN_TILE, T)
    nk = wo.shape[0] // FF_CHUNK
    vec = pl.BlockSpec((1, D), lambda b, i, k: (0, 0))
    per_b = pl.BlockSpec((1, 1, D), lambda b, i, k: (b, 0, 0))
    return pl.pallas_call(
        functools.partial(_ffn_kernel, final_norm=final_norm),
        grid=(B, T // tm, nk),
        in_specs=[
            pl.BlockSpec((1, tm, D), lambda b, i, k: (b, i, 0)),
            vec, per_b, per_b, per_b,
            pl.BlockSpec((D, 2 * FF_CHUNK), lambda b, i, k: (0, k)),
            pl.BlockSpec((FF_CHUNK, D), lambda b, i, k: (k, 0)),
            vec,
        ],
        out_specs=pl.BlockSpec((1, tm, D), lambda b, i, k: (b, i, 0)),
        out_shape=jax.ShapeDtypeStruct((B, T, D), F32),
        scratch_shapes=[pltpu.VMEM((tm, D), BF16), pltpu.VMEM((tm, D), F32)],
        compiler_params=_cparams(("arbitrary", "arbitrary", "arbitrary")),
    )(x, g, sh, sc, gt, wab, wo, g_final)


GATE_STEPS = N_BRANCH * D_MODEL // PROJ_CHUNK
FIRST_GATE_STEP = 8
DIL_SECTION = {d: d * BRANCH_WIDTH for d in DIL_DILATIONS}


def _rope_apply(z, ra, rb):
    return z * ra + pltpu.roll(z, LANES // 2, 1) * rb


def _inproj_kernel(x_ref, g_ref, sh_ref, sc_ref, w_ref, bg_ref, ra_ref, rb_ref,
                   tok_o, d4_o, d16_o, f_o, nq_o, rest_o, gate_o, h_scr, z_scr):
    j = pl.program_id(2)
    tm = h_scr.shape[0]
    n_sub = PROJ_CHUNK // MXU_COLS

    @pl.when(j == 0)
    def _():
        h = _norm_modulate(x_ref[0], g_ref[...], sh_ref[0], sc_ref[0])
        h_scr[...] = h.astype(BF16)

    def zsub(c):
        return _dot(h_scr[...], w_ref[:, c * MXU_COLS:(c + 1) * MXU_COLS])

    def roped(c):
        z = zsub(c)
        ra = ra_ref[...]
        rb = rb_ref[...]
        return [_rope_apply(z[:, u * LANES:(u + 1) * LANES], ra, rb) for u in range(MXU_COLS // LANES)]

    sub_groups = MXU_COLS // LANES
    dil_groups = BRANCH_WIDTH // LANES

    def scatter_residues():
        for d, o_ref, g0 in ((4, d4_o, 0), (16, d16_o, dil_groups)):
            for r in range(d):
                for u in range(dil_groups):
                    rows = z_scr[g0 + u, pl.ds(r, tm // d, stride=d), :]
                    lo = r * BRANCH_WIDTH + u * LANES
                    o_ref[0, :, lo:lo + LANES] = rows.astype(BF16)

    @pl.when((j == 0) | (j == 2))
    def _():
        for c in range(n_sub):
            for u, r in enumerate(roped(c)):
                lo = c * MXU_COLS + u * LANES
                tok_o[0, :, lo:lo + LANES] = r.astype(BF16)

    @pl.when((j == 1) | (j == 3))
    def _():
        for c in range(n_sub):
            for u, r in enumerate(roped(c)):
                z_scr[c * sub_groups + u] = r
        scatter_residues()

    @pl.when(j == 7)
    def _():
        for c in range(n_sub):
            z = zsub(c)
            for u in range(sub_groups):
                z_scr[c * sub_groups + u] = z[:, u * LANES:(u + 1) * LANES]
        scatter_residues()

    @pl.when(j == 4)
    def _():
        for c in range(n_sub):
            o_ref, lo = (f_o, c * MXU_COLS) if c < n_sub // 2 else (nq_o, (c - n_sub // 2) * MXU_COLS)
            o_ref[0, :, lo:lo + MXU_COLS] = zsub(c).astype(BF16)

    @pl.when((j == 5) | (j == 6))
    def _():
        for c in range(n_sub):
            rest_o[0, :, c * MXU_COLS:(c + 1) * MXU_COLS] = zsub(c).astype(BF16)

    @pl.when(j >= FIRST_GATE_STEP)
    def _():
        for c in range(n_sub):
            cols = slice(c * MXU_COLS, (c + 1) * MXU_COLS)
            pre = zsub(c) + bg_ref[:, cols]
            gate_o[0, :, cols] = (0.5 * jnp.tanh(0.5 * pre) + 0.5).astype(BF16)


def _inproj(x, g, sh, sc, w, bg, ra, rb):
    B, T, D = x.shape
    tm = min(TOKEN_TILE, T)
    n_steps = w.shape[1] // PROJ_CHUNK
    vec = pl.BlockSpec((1, D), lambda b, i, j: (0, 0))
    per_b = pl.BlockSpec((1, 1, D), lambda b, i, j: (b, 0, 0))
    tab = pl.BlockSpec((tm, LANES), lambda b, i, j: (i, 0))
    step = lambda j, *edges: sum((j >= e).astype(jnp.int32) for e in edges)
    dil_spec = lambda d: pl.BlockSpec((1, tm // d, DIL_SECTION[d]), lambda b, i, j: (b, i, step(j, 3, 7)))
    tok512 = pl.BlockSpec((1, tm, BRANCH_WIDTH), lambda b, i, j: (b, i, 0))
    return pl.pallas_call(
        _inproj_kernel,
        grid=(B, T // tm, n_steps),
        in_specs=[
            pl.BlockSpec((1, tm, D), lambda b, i, j: (b, i, 0)),
            vec, per_b, per_b,
            pl.BlockSpec((D, PROJ_CHUNK), lambda b, i, j: (0, j)),
            pl.BlockSpec((1, PROJ_CHUNK), lambda b, i, j: (0, jnp.maximum(j - FIRST_GATE_STEP, 0))),
            tab, tab,
        ],
        out_specs=[
            pl.BlockSpec((1, tm, PROJ_CHUNK), lambda b, i, j: (b, i, step(j, 2))),
            dil_spec(4), dil_spec(16),
            tok512, tok512,
            pl.BlockSpec((1, tm, PROJ_CHUNK), lambda b, i, j: (b, i, step(j, 6))),
            pl.BlockSpec((1, tm, PROJ_CHUNK), lambda b, i, j: (b, i, jnp.maximum(j - FIRST_GATE_STEP, 0))),
        ],
        out_shape=[
            jax.ShapeDtypeStruct((B, T, 2 * PROJ_CHUNK), BF16),
            jax.ShapeDtypeStruct((B, T // 4, 3 * DIL_SECTION[4]), BF16),
            jax.ShapeDtypeStruct((B, T // 16, 3 * DIL_SECTION[16]), BF16),
            jax.ShapeDtypeStruct((B, T, BRANCH_WIDTH), BF16),
            jax.ShapeDtypeStruct((B, T, BRANCH_WIDTH), BF16),
            jax.ShapeDtypeStruct((B, T, 2 * PROJ_CHUNK), BF16),
            jax.ShapeDtypeStruct((B, T, GATE_STEPS * PROJ_CHUNK), BF16),
        ],
        scratch_shapes=[pltpu.VMEM((tm, D), BF16), pltpu.VMEM((PROJ_CHUNK // LANES, tm, LANES), F32)],
        compiler_params=_cparams(("arbitrary", "arbitrary", "arbitrary")),
    )(x, g, sh, sc, w, bg, ra, rb)


FFT_T2 = 128
FFT_COLS = 2048
FFT_K1_BLOCK = 4


def _fft1_kernel(u_ref, c1_ref, s1_ref, ar_ref, ai_ref):
    u = u_ref[0]
    ar_ref[0] = _dot(c1_ref[...], u)
    ai_ref[0] = -_dot(s1_ref[...], u)


def _fft2_kernel(ar_ref, ai_ref, twc_ref, tws_ref, c2_ref, s2_ref, cc_ref, sc_ref, o_ref, *, norm):
    c2 = c2_ref[...]
    s2 = s2_ref[...]
    cc = cc_ref[...]
    sc = sc_ref[...]
    for kk in range(FFT_K1_BLOCK):
        rows = slice(kk * FFT_T2, (kk + 1) * FFT_T2)
        ar = ar_ref[0, rows, :]
        ai = ai_ref[0, rows, :]
        twc = jnp.concatenate([twc_ref[rows, :]] * (BRANCH_WIDTH // LANES), axis=1)
        tws = jnp.concatenate([tws_ref[rows, :]] * (BRANCH_WIDTH // LANES), axis=1)
        br = (ar * twc + ai * tws).astype(BF16)
        bi = (ai * twc - ar * tws).astype(BF16)
        zr = _dot(c2, br) + _dot(s2, bi)
        zi = _dot(c2, bi) - _dot(s2, br)
        y = _dot(zr.astype(BF16), cc) + _dot(zi.astype(BF16), sc)
        o_ref[0, :, kk * BRANCH_WIDTH:(kk + 1) * BRANCH_WIDTH] = (y * norm).astype(BF16)


def _dft_tables(T):
    T1 = T // FFT_T2
    k1 = np.arange(T1)
    a1 = 2.0 * np.pi * np.outer(k1, k1) / T1
    k2 = np.arange(FFT_T2)
    a2 = 2.0 * np.pi * np.outer(k2, k2) / FFT_T2
    atw = 2.0 * np.pi * np.outer(k1, k2).reshape(T, 1) / T
    atw = np.broadcast_to(atw, (T, LANES))
    ch = np.arange(FNET_GROUP_DIM)
    ac = 2.0 * np.pi * np.outer(ch, ch) / FNET_GROUP_DIM
    eye = np.eye(BRANCH_WIDTH // FNET_GROUP_DIM)
    bf = lambda a: jnp.asarray(a, dtype=BF16)
    return dict(c1=bf(np.cos(a1)), s1=bf(np.sin(a1)), c2=bf(np.cos(a2)), s2=bf(np.sin(a2)),
                twc=jnp.asarray(np.cos(atw), F32), tws=jnp.asarray(np.sin(atw), F32),
                cc=bf(np.kron(eye, np.cos(ac))), sc=bf(np.kron(eye, np.sin(ac))))


def _fourier_mix(f, tabs):
    B, T, C = f.shape
    T1 = T // FFT_T2
    n_col = FFT_T2 * C // FFT_COLS
    full2 = lambda shape: pl.BlockSpec(shape, lambda b, i: (0, 0))
    ar, ai = pl.pallas_call(
        _fft1_kernel,
        grid=(B, n_col),
        in_specs=[pl.BlockSpec((1, T1, FFT_COLS), lambda b, i: (b, 0, i)),
                  full2((T1, T1)), full2((T1, T1))],
        out_specs=[pl.BlockSpec((1, T1, FFT_COLS), lambda b, i: (b, 0, i))] * 2,
        out_shape=[jax.ShapeDtypeStruct((B, T1, FFT_T2 * C), F32)] * 2,
        compiler_params=_cparams(("arbitrary", "arbitrary")),
    )(f.reshape(B, T1, FFT_T2 * C), tabs["c1"], tabs["s1"])
    ar = ar.reshape(B, T, C)
    ai = ai.reshape(B, T, C)
    rows = FFT_K1_BLOCK * FFT_T2
    y = pl.pallas_call(
        functools.partial(_fft2_kernel, norm=1.0 / math.sqrt(T * FNET_GROUP_DIM)),
        grid=(B, T1 // FFT_K1_BLOCK),
        in_specs=[pl.BlockSpec((1, rows, C), lambda b, i: (b, i, 0)),
                  pl.BlockSpec((1, rows, C), lambda b, i: (b, i, 0)),
                  pl.BlockSpec((rows, LANES), lambda b, i: (i, 0)),
                  pl.BlockSpec((rows, LANES), lambda b, i: (i, 0)),
                  full2((FFT_T2, FFT_T2)), full2((FFT_T2, FFT_T2)),
                  full2((C, C)), full2((C, C))],
        out_specs=pl.BlockSpec((1, FFT_T2, FFT_K1_BLOCK * C), lambda b, i: (b, 0, i)),
        out_shape=jax.ShapeDtypeStruct((B, FFT_T2, T1 * C), BF16),
        compiler_params=_cparams(("arbitrary", "arbitrary")),
    )(ar, ai, tabs["twc"], tabs["tws"], tabs["c2"], tabs["s2"], tabs["cc"], tabs["sc"])
    return y.reshape(B, T, C)


NAT_Q_ROWS = 8
NAT_Q_TOK = NAT_Q_ROWS * GRID_W
NAT_EDGE_TOK = (NAT_WIN_H // 2) * GRID_W
NAT_WIN_TOK = NAT_Q_TOK + 2 * NAT_EDGE_TOK
NAT_SLAB_TOK = NAT_WIN_H * GRID_W


def _head_masks():
    lane = lax.broadcasted_iota(jnp.int32, (1, LANES), 1)
    return lane < HEAD_DIM


def _nat_kernel(q_ref, kp_ref, kc_ref, kn_ref, vp_ref, vc_ref, vn_ref, bias_ref, o_ref, kw, vw, *, rows):
    i = pl.program_id(1)
    kw[0:NAT_EDGE_TOK, :] = kp_ref[0]
    kw[NAT_EDGE_TOK:NAT_EDGE_TOK + NAT_Q_TOK, :] = kc_ref[0]
    kw[NAT_EDGE_TOK + NAT_Q_TOK:, :] = kn_ref[0]
    vw[0:NAT_EDGE_TOK, :] = vp_ref[0]
    vw[NAT_EDGE_TOK:NAT_EDGE_TOK + NAT_Q_TOK, :] = vc_ref[0]
    vw[NAT_EDGE_TOK + NAT_Q_TOK:, :] = vn_ref[0]
    first = _head_masks()
    r0 = i * NAT_Q_ROWS

    def row_body(rr, carry):
        r = r0 + rr
        rs = jnp.clip(r - NAT_WIN_H // 2, 0, rows - NAT_WIN_H)
        var = r - rs
        w0 = pl.multiple_of((rs - (r0 - NAT_WIN_H // 2)) * GRID_W, GRID_W)
        q0 = pl.multiple_of(rr * GRID_W, GRID_W)
        for hp in range(NAT_HEADS // 2):
            cols = slice(hp * LANES, (hp + 1) * LANES)
            q = q_ref[0, pl.ds(q0, GRID_W), cols]
            k = kw[pl.ds(w0, NAT_SLAB_TOK), cols]
            v = vw[pl.ds(w0, NAT_SLAB_TOK), cols]
            outs = []
            for hh in range(2):
                sel = first if hh == 0 else jnp.logical_not(first)
                qm = jnp.where(sel, q, jnp.zeros_like(q))
                s = _dot_nt(qm, k) + bias_ref[var, 2 * hp + hh]
                m = jnp.max(s, axis=-1, keepdims=True)
                p = jnp.exp2(s - m)
                l = jnp.sum(p, axis=-1, keepdims=True)
                outs.append(_dot(p.astype(BF16), v) / l)
            o = jnp.where(first, outs[0], outs[1])
            o_ref[0, pl.ds(q0, GRID_W), cols] = o.astype(BF16)
        return carry

    lax.fori_loop(0, NAT_Q_ROWS, row_body, 0)


def _nat_bias_table(rel_bias):
    H = rel_bias.shape[0]
    col = np.arange(GRID_W)
    col_start = np.clip(col - NAT_WIN_W // 2, 0, GRID_W - NAT_WIN_W)
    kc = np.arange(GRID_W)
    valid = (kc[None, :] >= col_start[:, None]) & (kc[None, :] < col_start[:, None] + NAT_WIN_W)
    col_off = np.clip(kc[None, :] - col[:, None] + (NAT_WIN_W - 1), 0, 2 * NAT_WIN_W - 2)
    d = np.arange(NAT_WIN_H)
    a = np.arange(NAT_WIN_H)
    row_off = a[None, :] - d[:, None] + (NAT_WIN_H - 1)
    tab = rel_bias[:, row_off][:, :, :, col_off]
    tab = jnp.where(jnp.asarray(valid)[None, None, None], tab.astype(F32) * LOG2E, NEG_INF)
    tab = jnp.transpose(tab, (1, 0, 3, 2, 4))
    return tab.reshape(NAT_WIN_H, H, GRID_W, NAT_SLAB_TOK)


def _neighborhood_attention(nq, rest, bias_tab):
    B, T, C = nq.shape
    rows = T // GRID_W
    n_blk = T // NAT_Q_TOK
    per = NAT_Q_TOK // NAT_EDGE_TOK
    n_edge = T // NAT_EDGE_TOK
    prev = lambda c: (lambda b, i: (b, jnp.maximum(i * per - 1, 0), c))
    cur = lambda c: (lambda b, i: (b, i, c))
    nxt = lambda c: (lambda b, i: (b, jnp.minimum((i + 1) * per, n_edge - 1), c))
    edge = lambda f: pl.BlockSpec((1, NAT_EDGE_TOK, C), f)
    mid = lambda f: pl.BlockSpec((1, NAT_Q_TOK, C), f)
    return pl.pallas_call(
        functools.partial(_nat_kernel, rows=rows),
        grid=(B, n_blk),
        in_specs=[mid(cur(0)),
                  edge(prev(0)), mid(cur(0)), edge(nxt(0)),
                  edge(prev(1)), mid(cur(1)), edge(nxt(1)),
                  pl.BlockSpec(bias_tab.shape, lambda b, i: (0, 0, 0, 0))],
        out_specs=mid(cur(0)),
        out_shape=jax.ShapeDtypeStruct((B, T, C), BF16),
        scratch_shapes=[pltpu.VMEM((NAT_WIN_TOK, C), BF16), pltpu.VMEM((NAT_WIN_TOK, C), BF16)],
        compiler_params=_cparams(("arbitrary", "arbitrary")),
    )(nq, rest, rest, rest, rest, rest, rest, bias_tab)


DIFF_TQ = 512
DIFF_TK = 512
DIFF_UNROLL = 4
DIFF_HEADROOM = 64.0


def _qk_head_mask():
    lane = lax.broadcasted_iota(jnp.int32, (1, LANES), 1)
    return (lane % (LANES // 2)) < HEAD_DIM // 2


def _diff_kernel(q_ref, k_ref, v_ref, lq1_ref, lk1_ref, lq2_ref, lk2_ref, g_ref, o_ref, a1_scr, a2_scr,
                 *, lam_init, n_kv):
    first = _qk_head_mask()
    q = q_ref[0]
    zero = jnp.zeros_like(q)
    q1 = jnp.where(first, q, zero)
    q2 = jnp.where(first, zero, q)
    tq = q.shape[0]
    ones = jnp.ones((DIFF_TK, LANES), BF16)
    zacc = jnp.zeros((tq, 2 * LANES), F32)

    def tile(c):
        k0 = pl.multiple_of(c * DIFF_TK, DIFF_TK)
        k = k_ref[0, pl.ds(k0, DIFF_TK), :]
        v1 = jnp.concatenate([v_ref[0, pl.ds(k0, DIFF_TK), :], ones], axis=1)
        return k, v1

    k, _ = tile(0)
    r1 = jnp.max(_dot_nt(q1, k), axis=-1, keepdims=True)
    r2 = jnp.max(_dot_nt(q2, k), axis=-1, keepdims=True)

    def fast(c, carry):
        a1, a2, t1, t2 = carry
        k, v1 = tile(c)

        def one(qm, r, a, t):
            s = _dot_nt(qm, k)
            for u in range(DIFF_TK // LANES):
                t = jnp.maximum(t, s[:, u * LANES:(u + 1) * LANES])
            return a + _dot(jnp.exp2(s - r).astype(BF16), v1), t

        a1, t1 = one(q1, r1, a1, t1)
        a2, t2 = one(q2, r2, a2, t2)
        return a1, a2, t1, t2

    tneg = jnp.full((tq, LANES), NEG_INF, F32)
    a1, a2, t1, t2 = lax.fori_loop(0, n_kv, fast, (zacc, zacc, tneg, tneg), unroll=DIFF_UNROLL)
    a1_scr[...] = a1
    a2_scr[...] = a2
    growth = jnp.max(jnp.maximum(t1 - r1, t2 - r2))

    @pl.when(jnp.logical_not(growth <= DIFF_HEADROOM))
    def _():
        def slow(c, carry):
            m1, b1, m2, b2 = carry
            k, v1 = tile(c)

            def one(qm, m, a):
                s = _dot_nt(qm, k)
                mn = jnp.maximum(m, jnp.max(s, axis=-1, keepdims=True))
                return mn, jnp.exp2(m - mn) * a + _dot(jnp.exp2(s - mn).astype(BF16), v1)

            m1, b1 = one(q1, m1, b1)
            m2, b2 = one(q2, m2, b2)
            return m1, b1, m2, b2

        neg = jnp.full((tq, 1), NEG_INF, F32)
        _, b1, _, b2 = lax.fori_loop(0, n_kv, slow, (neg, zacc, neg, zacc))
        a1_scr[...] = b1
        a2_scr[...] = b2

    lam = (jnp.exp(jnp.sum(lq1_ref[...] * lk1_ref[...], keepdims=True))
           - jnp.exp(jnp.sum(lq2_ref[...] * lk2_ref[...], keepdims=True)) + lam_init)
    o = (a1_scr[:, :LANES] / a1_scr[:, LANES:]) - lam * (a2_scr[:, :LANES] / a2_scr[:, LANES:])
    ms = jnp.mean(o * o, axis=-1, keepdims=True)
    o = o * lax.rsqrt(ms + EPS) * g_ref[...]
    o_ref[0] = (o * (1.0 - lam_init)).astype(BF16)


def _diff_attention(tok, rest, lq1, lk1, lq2, lk2, ln_g, lam_init):
    B, T, _ = tok.shape
    tq = min(DIFF_TQ, T)
    k_blk = 1024 // LANES
    v_blk = 1024 // LANES
    vec = lambda n: pl.BlockSpec((1, n), lambda b, h, i: (0, 0))
    return pl.pallas_call(
        functools.partial(_diff_kernel, lam_init=lam_init, n_kv=T // DIFF_TK),
        grid=(B, DIFF_HEADS, T // tq),
        in_specs=[pl.BlockSpec((1, tq, LANES), lambda b, h, i: (b, i, h)),
                  pl.BlockSpec((1, T, LANES), lambda b, h, i: (b, 0, k_blk + h)),
                  pl.BlockSpec((1, T, LANES), lambda b, h, i: (b, 0, v_blk + h)),
                  vec(HEAD_DIM), vec(HEAD_DIM), vec(HEAD_DIM), vec(HEAD_DIM), vec(LANES)],
        out_specs=pl.BlockSpec((1, tq, LANES), lambda b, h, i: (b, i, h)),
        out_shape=jax.ShapeDtypeStruct((B, T, DIFF_HEADS * LANES), BF16),
        scratch_shapes=[pltpu.VMEM((tq, 2 * LANES), F32), pltpu.VMEM((tq, 2 * LANES), F32)],
        compiler_params=_cparams(("arbitrary", "arbitrary", "arbitrary")),
    )(tok, tok, rest, lq1, lk1, lq2, lk2, ln_g)


DIL_TQ = 512
DIL_EDGE = 128


def _dil_kernel(q_ref, kp_ref, kc_ref, kn_ref, vp_ref, vc_ref, vn_ref, o_ref, lse_ref, kw, vw, *, seq, tq):
    i = pl.program_id(2)
    kw[0:DIL_EDGE, :] = kp_ref[0]
    kw[DIL_EDGE:DIL_EDGE + tq, :] = kc_ref[0]
    kw[DIL_EDGE + tq:, :] = kn_ref[0]
    vw[0:DIL_EDGE, :] = vp_ref[0]
    vw[DIL_EDGE:DIL_EDGE + tq, :] = vc_ref[0]
    vw[DIL_EDGE + tq:, :] = vn_ref[0]
    win = tq + 2 * DIL_EDGE
    s0 = i * tq
    qpos = s0 + lax.broadcasted_iota(jnp.int32, (tq, win), 0)
    kpos = s0 - DIL_EDGE + lax.broadcasted_iota(jnp.int32, (tq, win), 1)
    valid = (jnp.abs(kpos - qpos) <= DIL_SIDE) & (kpos >= 0) & (kpos < seq)
    first_qk = _qk_head_mask()
    first_v = _head_masks()
    for hp in range(DIL_HEADS // 2):
        cols = slice(hp * LANES, (hp + 1) * LANES)
        q = q_ref[0, :, cols]
        k = kw[:, cols]
        v = vw[:, cols]
        outs, lses = [], []
        for hh in range(2):
            sel = first_qk if hh == 0 else jnp.logical_not(first_qk)
            qm = jnp.where(sel, q, jnp.zeros_like(q))
            s = jnp.where(valid, _dot_nt(qm, k), NEG_INF)
            m = jnp.max(s, axis=-1, keepdims=True)
            p = jnp.exp2(s - m)
            l = jnp.sum(p, axis=-1, keepdims=True)
            outs.append(_dot(p.astype(BF16), v) / l)
            lses.append(m + jnp.log2(l))
        o_ref[0, :, cols] = jnp.where(first_v, outs[0], outs[1]).astype(BF16)
        lse_ref[0, :, cols] = jnp.where(first_v, lses[0], lses[1])


def _dilated_group(qa, ka, va, qc, kc, vc, dil):
    B, seq, _ = qa.shape
    C = DIL_HEADS * HEAD_DIM
    tq = min(DIL_TQ, seq)
    per = tq // DIL_EDGE
    n_edge = seq // DIL_EDGE
    prev = lambda cf: (lambda b, r, i: (b, jnp.maximum(i * per - 1, 0), cf(r)))
    cur = lambda cf: (lambda b, r, i: (b, i, cf(r)))
    nxt = lambda cf: (lambda b, r, i: (b, jnp.minimum((i + 1) * per, n_edge - 1), cf(r)))
    edge = lambda f: pl.BlockSpec((1, DIL_EDGE, C), f)
    mid = lambda f: pl.BlockSpec((1, tq, C), f)
    out_map = lambda b, r, i: (b, i, r)
    return pl.pallas_call(
        functools.partial(_dil_kernel, seq=seq, tq=tq),
        grid=(B, dil, seq // tq),
        in_specs=[mid(cur(qc)),
                  edge(prev(kc)), mid(cur(kc)), edge(nxt(kc)),
                  edge(prev(vc)), mid(cur(vc)), edge(nxt(vc))],
        out_specs=[pl.BlockSpec((1, tq, C), out_map), pl.BlockSpec((1, tq, C), out_map)],
        out_shape=[jax.ShapeDtypeStruct((B, seq, dil * C), BF16),
                   jax.ShapeDtypeStruct((B, seq, dil * C), F32)],
        scratch_shapes=[pltpu.VMEM((tq + 2 * DIL_EDGE, C), BF16), pltpu.VMEM((tq + 2 * DIL_EDGE, C), BF16)],
        compiler_params=_cparams(("arbitrary", "arbitrary", "arbitrary")),
    )(qa, ka, ka, ka, va, va, va)


MERGE_CHUNK = 512


def _merge_kernel(x_ref, gt_ref, ya_ref, yb_ref, yc_ref, o0_ref, o1_ref, o2_ref, s0_ref, s1_ref, s2_ref,
                  g0_ref, g1_ref, g2_ref, g3_ref, wb_ref, wo_ref, out_ref,
                  yd_scr, acc_scr, o1_scr, s1_scr, o2_scr, s2_scr):
    j = pl.program_id(2)
    tm = acc_scr.shape[0]
    C = BRANCH_WIDTH

    @pl.when(j == 0)
    def _():
        for d, o_ref, s_ref, o_scr, s_scr in ((4, o1_ref, s1_ref, o1_scr, s1_scr),
                                              (16, o2_ref, s2_ref, o2_scr, s2_scr)):
            for r in range(d):
                for u in range(C // LANES):
                    cols = slice(r * C + u * LANES, r * C + (u + 1) * LANES)
                    o_scr[u, pl.ds(r, tm // d, stride=d), :] = o_ref[0, :, cols].astype(F32)
                    s_scr[u, pl.ds(r, tm // d, stride=d), :] = s_ref[0, :, cols]
        wide = lambda scr: jnp.concatenate([scr[u] for u in range(C // LANES)], axis=1)
        s0 = s0_ref[0]
        s1 = wide(s1_scr)
        s2 = wide(s2_scr)
        m = jnp.maximum(jnp.maximum(s0, s1), s2)
        e0 = jnp.exp2(s0 - m)
        e1 = jnp.exp2(s1 - m)
        e2 = jnp.exp2(s2 - m)
        num = o0_ref[0].astype(F32) * e0 + wide(o1_scr) * e1 + wide(o2_scr) * e2
        yd_scr[...] = (num / (e0 + e1 + e2)).astype(BF16)
        acc_scr[...] = jnp.zeros_like(acc_scr)

    merged = g0_ref[0].astype(F32) * _dot(ya_ref[0], wb_ref[0])
    merged += g1_ref[0].astype(F32) * _dot(yb_ref[0], wb_ref[1])
    merged += g2_ref[0].astype(F32) * _dot(yc_ref[0], wb_ref[2])
    merged += g3_ref[0].astype(F32) * _dot(yd_scr[...], wb_ref[3])
    acc_scr[...] += _dot(merged.astype(BF16), wo_ref[...])

    @pl.when(j == pl.num_programs(2) - 1)
    def _():
        out_ref[0] = x_ref[0] + gt_ref[0] * acc_scr[...]


def _merge(x, gt, ya, yb, yc, dil_o, dil_lse, gates, wb, wo):
    B, T, D = x.shape
    tm = min(TOKEN_TILE, T)
    C = BRANCH_WIDTH
    n_j = D // MERGE_CHUNK
    tok = pl.BlockSpec((1, tm, C), lambda b, i, j: (b, i, 0))
    res = lambda d: pl.BlockSpec((1, tm // d, d * C), lambda b, i, j: (b, i, 0))
    gate = lambda n: pl.BlockSpec((1, tm, MERGE_CHUNK), lambda b, i, j: (b, i, n * n_j + j))
    dil_specs = [res(d) for d in DIL_DILATIONS]
    return pl.pallas_call(
        _merge_kernel,
        grid=(B, T // tm, n_j),
        in_specs=[pl.BlockSpec((1, tm, D), lambda b, i, j: (b, i, 0)),
                  pl.BlockSpec((1, 1, D), lambda b, i, j: (b, 0, 0)),
                  tok, tok, tok, *dil_specs, *dil_specs,
                  gate(0), gate(1), gate(2), gate(3),
                  pl.BlockSpec((N_BRANCH, C, MERGE_CHUNK), lambda b, i, j: (0, 0, j)),
                  pl.BlockSpec((MERGE_CHUNK, D), lambda b, i, j: (j, 0))],
        out_specs=pl.BlockSpec((1, tm, D), lambda b, i, j: (b, i, 0)),
        out_shape=jax.ShapeDtypeStruct((B, T, D), F32),
        scratch_shapes=[pltpu.VMEM((tm, C), BF16), pltpu.VMEM((tm, D), F32)]
                       + [pltpu.VMEM((C // LANES, tm, LANES), F32)] * 4,
        compiler_params=_cparams(("arbitrary", "arbitrary", "arbitrary")),
    )(x, gt, ya, yb, yc, *dil_o, *dil_lse, gates, gates, gates, gates, wb, wo)


def _prep_ffn(w_in, w_out):
    D = w_in.shape[0]
    pad = D_FF_PAD - D_FF
    a = jnp.pad(w_in[:, :D_FF].astype(BF16), ((0, 0), (0, pad)))
    b = jnp.pad(w_in[:, D_FF:].astype(BF16), ((0, 0), (0, pad)))
    n = D_FF_PAD // FF_CHUNK
    wab = jnp.concatenate([a.reshape(D, n, FF_CHUNK), b.reshape(D, n, FF_CHUNK)], axis=2)
    wo = jnp.pad(w_out.astype(BF16), ((0, pad), (0, 0)))
    return wab.reshape(D, n * 2 * FF_CHUNK), wo


def _qk_lane_order():
    rot_half = ROT_DIM // 2
    rest_half = (HEAD_DIM - ROT_DIM) // 2
    idx = []
    for n in range(LANES):
        half, w = divmod(n, LANES // 2)
        head, i = divmod(w, HEAD_DIM // 2)
        d = i + rot_half * half if i < rot_half else ROT_DIM + (i - rot_half) + rest_half * half
        idx.append(head * HEAD_DIM + d)
    return np.asarray(idx)


def _prep_w_in(w):
    W = BRANCH_WIDTH
    D = w.shape[0]
    qscale = HEAD_DIM ** -0.5 * LOG2E
    order = _qk_lane_order()
    qk = lambda cols: cols.reshape(D, -1, LANES)[:, :, order].reshape(D, -1)
    f_in, nq, nk, nv, dq, dk, dv = [w[:, n * W:(n + 1) * W] for n in range(7)]
    lq = [qk(w[:, (7 + g) * W:(8 + g) * W] * qscale) for g in range(3)]
    lk = [qk(w[:, (10 + g) * W:(11 + g) * W]) for g in range(3)]
    lv = [w[:, (13 + g) * W:(14 + g) * W] for g in range(3)]
    gates = w[:, MIX_COLS:]
    cols = [qk(dq * qscale), lq[0], lq[1], lq[2], qk(dk), lk[0], lk[1], lk[2],
            f_in, nq * qscale, nk, nv, dv, lv[0], lv[1], lv[2], gates]
    return jnp.concatenate(cols, axis=1).astype(BF16)


def _rope_tables(T):
    rot_half = ROT_DIM // 2
    inv = ROPE_THETA ** (-jnp.arange(0, ROT_DIM, 2, dtype=F32) / ROT_DIM)
    ang = jnp.arange(T, dtype=F32)[:, None] * inv[None, :]
    cos, sin = jnp.cos(ang), jnp.sin(ang)
    lane = np.arange(LANES)
    i = lane % (HEAD_DIM // 2)
    is_rot = jnp.asarray(i < rot_half)[None, :]
    src = np.minimum(i, rot_half - 1)
    sign = jnp.asarray(np.where(lane < LANES // 2, -1.0, 1.0), F32)[None, :]
    ra = jnp.where(is_rot, cos[:, src], 1.0)
    rb = jnp.where(is_rot, sin[:, src] * sign, 0.0)
    return ra, rb


def _trunk(x, mods, P, tabs):
    B, T, D = x.shape
    for l in range(DEPTH):
        sh1, sc1, gt1, sh2, sc2, gt2, sh3, sc3, gt3 = [mods[l][:, i:i + 1, :] for i in range(N_MOD)]
        x = _ffn(x, P["g_ffn1"][l], sh1, sc1, gt1, P["wab1"][l], P["wo1"][l], P["g_final"], False)
        tok, d4, d16, f_in, nq, rest, gates = _inproj(x, P["g_mix"][l], sh2, sc2, P["w_in"][l],
                                                      P["b_gate"][l], *tabs["rope"])
        ya = _fourier_mix(f_in, tabs["dft"])
        yb = _neighborhood_attention(nq, rest, P["nat_bias"][l])
        lam_init = 0.8 - 0.6 * math.exp(-0.3 * l)
        yc = _diff_attention(tok, rest, P["lam_q1"][l], P["lam_k1"][l], P["lam_q2"][l], P["lam_k2"][l],
                             P["diff_ln_g"][l], lam_init)
        dil = [_dilated_group(tok, tok, rest, lambda r: 1, lambda r: 3, lambda r: 3, 1),
               _dilated_group(d4, d4, d4, lambda r: r, lambda r: 4 + r, lambda r: 8 + r, 4),
               _dilated_group(d16, d16, d16, lambda r: r, lambda r: 16 + r, lambda r: 32 + r, 16)]
        x = _merge(x, gt2, ya, yb, yc, [o for o, _ in dil], [s for _, s in dil], gates,
                   P["w_branch"][l], P["w_out"][l])
        x = _ffn(x, P["g_ffn2"][l], sh3, sc3, gt3, P["wab2"][l], P["wo2"][l], P["g_final"], l == DEPTH - 1)
    return x


def kernel(x_prompt, x_sample, c_prompt, c_sample, w_ada, b_ada, g_ffn1, w_ffn1_in, w_ffn1_out, g_mix, w_in, b_gate, nat_rel_bias, lam_q1, lam_k1, lam_q2, lam_k2, diff_ln_g, w_branch, w_out, g_ffn2, w_ffn2_in, w_ffn2_out, g_final):
    L = DEPTH
    D = D_MODEL
    Bp, Bs = c_prompt.shape[0], c_sample.shape[0]
    rows = ((Bp + Bs + 7) // 8) * 8
    c_all = jnp.concatenate([c_prompt, c_sample, jnp.zeros((rows - Bp - Bs, D), F32)], axis=0)
    mods = _modulation(c_all, w_ada, b_ada).reshape(L, rows, N_MOD, D)

    P = {}
    ffn1 = [_prep_ffn(w_ffn1_in[l], w_ffn1_out[l]) for l in range(L)]
    ffn2 = [_prep_ffn(w_ffn2_in[l], w_ffn2_out[l]) for l in range(L)]
    P["wab1"] = [a for a, _ in ffn1]
    P["wo1"] = [b for _, b in ffn1]
    P["wab2"] = [a for a, _ in ffn2]
    P["wo2"] = [b for _, b in ffn2]
    P["w_in"] = [_prep_w_in(w_in[l]) for l in range(L)]
    P["b_gate"] = [b_gate[l].reshape(1, N_BRANCH * D) for l in range(L)]
    P["nat_bias"] = [_nat_bias_table(nat_rel_bias[l]) for l in range(L)]
    P["w_branch"] = [w_branch[l].astype(BF16) for l in range(L)]
    P["w_out"] = [w_out[l].astype(BF16) for l in range(L)]
    for name, arr in (("g_ffn1", g_ffn1), ("g_mix", g_mix), ("g_ffn2", g_ffn2), ("lam_q1", lam_q1),
                      ("lam_k1", lam_k1), ("lam_q2", lam_q2), ("lam_k2", lam_k2), ("diff_ln_g", diff_ln_g)):
        P[name] = [arr[l].reshape(1, -1) for l in range(L)]
    P["g_final"] = g_final.reshape(1, D)

    outs = []
    for x, lo, hi in ((x_prompt, 0, Bp), (x_sample, Bp, Bp + Bs)):
        T = x.shape[1]
        tabs = {"rope": _rope_tables(T), "dft": _dft_tables(T)}
        outs.append(_trunk(x, mods[:, lo:hi], P, tabs))
    return tuple(outs)
```

```python
import functools
import math

import numpy as np
import jax
import jax.numpy as jnp
from jax import lax
from jax.experimental import pallas as pl
from jax.experimental.pallas import tpu as pltpu

F32 = jnp.float32
BF16 = jnp.bfloat16

D_MODEL = 2048
DEPTH = 2
GRID_W = 64
HEAD_DIM = 64
ROT_DIM = HEAD_DIM // 4
ROPE_THETA = 500000.0
EPS = 1e-6
HALF = 0.5
FNET_GROUP_DIM = 128
NAT_HEADS = 8
NAT_WIN_H = 8
NAT_WIN_W = 16
DIFF_HEADS = 4
DIL_DILATIONS = (1, 4, 16)
DIL_SIDE = 64
DIL_HEADS = 8
N_BRANCH = 4
BRANCH_WIDTH = 512
N_MOD = 9
D_FF = ((8 * D_MODEL // 3 + 63) // 64) * 64
MIX_COLS = 8192

LANES = 128
VMEM_LIMIT = 56 * 1024 * 1024
FF_CHUNK = 512
D_FF_PAD = ((D_FF + FF_CHUNK - 1) // FF_CHUNK) * FF_CHUNK
TOKEN_TILE = 1024
MERGE_TILE = 512
PROJ_CHUNK = 1024
MXU_COLS = 256
NEG_INF = -1e30
LOG2E = math.log2(math.e)


def _cparams(sem):
    return pltpu.CompilerParams(dimension_semantics=sem, vmem_limit_bytes=VMEM_LIMIT)


def _dot(a, b):
    return jnp.dot(a, b, preferred_element_type=F32)


def _dot_nt(a, b):
    return lax.dot_general(a, b, (((1,), (1,)), ((), ())), preferred_element_type=F32)


def _norm_modulate(x, g, sh, sc):
    ms = jnp.mean(x * x, axis=-1, keepdims=True)
    y = x * lax.rsqrt(ms + EPS) * g
    return y * (1.0 + sc) + sh


def _mod_kernel(c_ref, w_ref, b_ref, o_ref):
    c = c_ref[...]
    s = c * jax.nn.sigmoid(c)
    o_ref[0] = _dot(s.astype(BF16), w_ref[0].astype(BF16)) + b_ref[0]


def _modulation(c_all, w_ada, b_ada):
    L, D, N = w_ada.shape
    R = c_all.shape[0]
    tn = 1152
    return pl.pallas_call(
        _mod_kernel,
        grid=(L, N // tn),
        in_specs=[
            pl.BlockSpec((R, D), lambda l, j: (0, 0)),
            pl.BlockSpec((1, D, tn), lambda l, j: (l, 0, j)),
            pl.BlockSpec((1, 1, tn), lambda l, j: (l, 0, j)),
        ],
        out_specs=pl.BlockSpec((1, R, tn), lambda l, j: (l, 0, j)),
        out_shape=jax.ShapeDtypeStruct((L, R, N), F32),
        compiler_params=_cparams(("arbitrary", "arbitrary")),
    )(c_all, w_ada, b_ada.reshape(L, 1, N))


def _ffn_kernel(x_ref, g_ref, sh_ref, sc_ref, gt_ref, wab_ref, wo_ref, gf_ref, o_ref, h_scr, acc_scr,
                *, final_norm):
    k = pl.program_id(2)

    @pl.when(k == 0)
    def _():
        h = _norm_modulate(x_ref[0], g_ref[...], sh_ref[0], sc_ref[0])
        h_scr[...] = h.astype(BF16)
        acc_scr[...] = jnp.zeros_like(acc_scr)

    ab = _dot(h_scr[...], wab_ref[...])
    a = ab[:, :FF_CHUNK]
    b = ab[:, FF_CHUNK:]
    act = (a * jax.nn.sigmoid(a)) * b
    acc_scr[...] += _dot(act.astype(BF16), wo_ref[...])

    @pl.when(k == pl.num_programs(2) - 1)
    def _():
        xn = x_ref[0] + (HALF * gt_ref[0]) * acc_scr[...]
        if final_norm:
            ms = jnp.mean(xn * xn, axis=-1, keepdims=True)
            xn = xn * lax.rsqrt(ms + EPS) * gf_ref[...]
        o_ref[0] = xn


def _ffn(x, g, sh, sc, gt, wab, wo, g_final, final_norm):
    B, T, D = x.shape
    tm = min(TOKEN_TILE, T)
    nk = wo.shape[0] // FF_CHUNK
    vec = pl.BlockSpec((1, D), lambda b, i, k: (0, 0))
    per_b = pl.BlockSpec((1, 1, D), lambda b, i, k: (b, 0, 0))
    return pl.pallas_call(
        functools.partial(_ffn_kernel, final_norm=final_norm),
        grid=(B, T // tm, nk),
        in_specs=[
            pl.BlockSpec((1, tm, D), lambda b, i, k: (b, i, 0), pipeline_mode=pl.Buffered(1)),
            vec, per_b, per_b, per_b,
            pl.BlockSpec((D, 2 * FF_CHUNK), lambda b, i, k: (0, k)),
            pl.BlockSpec((FF_CHUNK, D), lambda b, i, k: (k, 0)),
            vec,
        ],
        out_specs=pl.BlockSpec((1, tm, D), lambda b, i, k: (b, i, 0), pipeline_mode=pl.Buffered(1)),
        out_shape=jax.ShapeDtypeStruct((B, T, D), F32),
        scratch_shapes=[pltpu.VMEM((tm, D), BF16), pltpu.VMEM((tm, D), F32)],
        compiler_params=_cparams(("arbitrary", "arbitrary", "arbitrary")),
    )(x, g, sh, sc, gt, wab, wo, g_final)


GATE_STEPS = N_BRANCH * D_MODEL // PROJ_CHUNK
FIRST_GATE_STEP = 8
DIL_SECTION = {d: d * BRANCH_WIDTH for d in DIL_DILATIONS}


def _rope_apply(z, ra, rb):
    return z * ra + pltpu.roll(z, LANES // 2, 1) * rb


def _inproj_kernel(x_ref, g_ref, sh_ref, sc_ref, w_ref, bg_ref, ra_ref, rb_ref,
                   tok_o, d4_o, d16_o, f_o, nq_o, rest_o, gate_o, h_scr, z_scr):
    j = pl.program_id(2)
    tm = h_scr.shape[0]
    n_sub = PROJ_CHUNK // MXU_COLS

    @pl.when(j == 0)
    def _():
        h = _norm_modulate(x_ref[0], g_ref[...], sh_ref[0], sc_ref[0])
        h_scr[...] = h.astype(BF16)

    def zsub(c):
        return _dot(h_scr[...], w_ref[:, c * MXU_COLS:(c + 1) * MXU_COLS])

    def roped(c):
        z = zsub(c)
        ra = ra_ref[...]
        rb = rb_ref[...]
        return [_rope_apply(z[:, u * LANES:(u + 1) * LANES], ra, rb) for u in range(MXU_COLS // LANES)]

    sub_groups = MXU_COLS // LANES
    dil_groups = BRANCH_WIDTH // LANES

    def scatter_residues():
        for d, o_ref, g0 in ((4, d4_o, 0), (16, d16_o, dil_groups)):
            for r in range(d):
                for u in range(dil_groups):
                    rows = z_scr[g0 + u, pl.ds(r, tm // d, stride=d), :]
                    lo = r * BRANCH_WIDTH + u * LANES
                    o_ref[0, :, lo:lo + LANES] = rows.astype(BF16)

    @pl.when((j == 0) | (j == 2))
    def _():
        for c in range(n_sub):
            for u, r in enumerate(roped(c)):
                lo = c * MXU_COLS + u * LANES
                tok_o[0, :, lo:lo + LANES] = r.astype(BF16)

    @pl.when((j == 1) | (j == 3))
    def _():
        for c in range(n_sub):
            for u, r in enumerate(roped(c)):
                z_scr[c * sub_groups + u] = r
        scatter_residues()

    @pl.when(j == 7)
    def _():
        for c in range(n_sub):
            z = zsub(c)
            for u in range(sub_groups):
                z_scr[c * sub_groups + u] = z[:, u * LANES:(u + 1) * LANES]
        scatter_residues()

    @pl.when(j == 4)
    def _():
        for c in range(n_sub):
            o_ref, lo = (f_o, c * MXU_COLS) if c < n_sub // 2 else (nq_o, (c - n_sub // 2) * MXU_COLS)
            o_ref[0, :, lo:lo + MXU_COLS] = zsub(c).astype(BF16)

    @pl.when((j == 5) | (j == 6))
    def _():
        for c in range(n_sub):
            rest_o[0, :, c * MXU_COLS:(c + 1) * MXU_COLS] = zsub(c).astype(BF16)

    @pl.when(j >= FIRST_GATE_STEP)
    def _():
        for c in range(n_sub):
            cols = slice(c * MXU_COLS, (c + 1) * MXU_COLS)
            pre = zsub(c) + bg_ref[:, cols]
            gate_o[0, :, cols] = (0.5 * jnp.tanh(0.5 * pre) + 0.5).astype(BF16)


def _inproj(x, g, sh, sc, w, bg, ra, rb):
    B, T, D = x.shape
    tm = min(TOKEN_TILE, T)
    n_steps = w.shape[1] // PROJ_CHUNK
    vec = pl.BlockSpec((1, D), lambda b, i, j: (0, 0))
    per_b = pl.BlockSpec((1, 1, D), lambda b, i, j: (b, 0, 0))
    tab = pl.BlockSpec((tm, LANES), lambda b, i, j: (i, 0))
    step = lambda j, *edges: sum((j >= e).astype(jnp.int32) for e in edges)
    dil_spec = lambda d: pl.BlockSpec((1, tm // d, DIL_SECTION[d]), lambda b, i, j: (b, i, step(j, 3, 7)))
    tok512 = pl.BlockSpec((1, tm, BRANCH_WIDTH), lambda b, i, j: (b, i, 0))
    return pl.pallas_call(
        _inproj_kernel,
        grid=(B, T // tm, n_steps),
        in_specs=[
            pl.BlockSpec((1, tm, D), lambda b, i, j: (b, i, 0), pipeline_mode=pl.Buffered(1)),
            vec, per_b, per_b,
            pl.BlockSpec((D, PROJ_CHUNK), lambda b, i, j: (0, j)),
            pl.BlockSpec((1, PROJ_CHUNK), lambda b, i, j: (0, jnp.maximum(j - FIRST_GATE_STEP, 0))),
            tab, tab,
        ],
        out_specs=[
            pl.BlockSpec((1, tm, PROJ_CHUNK), lambda b, i, j: (b, i, step(j, 2))),
            dil_spec(4), dil_spec(16),
            tok512, tok512,
            pl.BlockSpec((1, tm, PROJ_CHUNK), lambda b, i, j: (b, i, step(j, 6))),
            pl.BlockSpec((1, tm, PROJ_CHUNK), lambda b, i, j: (b, i, jnp.maximum(j - FIRST_GATE_STEP, 0))),
        ],
        out_shape=[
            jax.ShapeDtypeStruct((B, T, 2 * PROJ_CHUNK), BF16),
            jax.ShapeDtypeStruct((B, T // 4, 3 * DIL_SECTION[4]), BF16),
            jax.ShapeDtypeStruct((B, T // 16, 3 * DIL_SECTION[16]), BF16),
            jax.ShapeDtypeStruct((B, T, BRANCH_WIDTH), BF16),
            jax.ShapeDtypeStruct((B, T, BRANCH_WIDTH), BF16),
            jax.ShapeDtypeStruct((B, T, 2 * PROJ_CHUNK), BF16),
            jax.ShapeDtypeStruct((B, T, GATE_STEPS * PROJ_CHUNK), BF16),
        ],
        scratch_shapes=[pltpu.VMEM((tm, D), BF16), pltpu.VMEM((PROJ_CHUNK // LANES, tm, LANES), F32)],
        compiler_params=_cparams(("arbitrary", "arbitrary", "arbitrary")),
    )(x, g, sh, sc, w, bg, ra, rb)


FFT_T2 = 128
FFT_COLS = 2048
FFT_K1_BLOCK = 4


def _fft1_kernel(u_ref, c1_ref, s1_ref, ar_ref, ai_ref):
    u = u_ref[0]
    ar_ref[0] = _dot(c1_ref[...], u)
    ai_ref[0] = -_dot(s1_ref[...], u)


def _fft2_kernel(ar_ref, ai_ref, twc_ref, tws_ref, c2_ref, s2_ref, cc_ref, sc_ref, o_ref, *, norm):
    c2 = c2_ref[...]
    s2 = s2_ref[...]
    cc = cc_ref[...]
    sc = sc_ref[...]
    for kk in range(FFT_K1_BLOCK):
        rows = slice(kk * FFT_T2, (kk + 1) * FFT_T2)
        ar = ar_ref[0, rows, :]
        ai = ai_ref[0, rows, :]
        twc = jnp.concatenate([twc_ref[rows, :]] * (BRANCH_WIDTH // LANES), axis=1)
        tws = jnp.concatenate([tws_ref[rows, :]] * (BRANCH_WIDTH // LANES), axis=1)
        br = (ar * twc + ai * tws).astype(BF16)
        bi = (ai * twc - ar * tws).astype(BF16)
        zr = _dot(c2, br) + _dot(s2, bi)
        zi = _dot(c2, bi) - _dot(s2, br)
        y = _dot(zr.astype(BF16), cc) + _dot(zi.astype(BF16), sc)
        o_ref[0, :, kk * BRANCH_WIDTH:(kk + 1) * BRANCH_WIDTH] = (y * norm).astype(BF16)


def _dft_tables(T):
    T1 = T // FFT_T2
    k1 = np.arange(T1)
    a1 = 2.0 * np.pi * np.outer(k1, k1) / T1
    k2 = np.arange(FFT_T2)
    a2 = 2.0 * np.pi * np.outer(k2, k2) / FFT_T2
    atw = 2.0 * np.pi * np.outer(k1, k2).reshape(T, 1) / T
    atw = np.broadcast_to(atw, (T, LANES))
    ch = np.arange(FNET_GROUP_DIM)
    ac = 2.0 * np.pi * np.outer(ch, ch) / FNET_GROUP_DIM
    eye = np.eye(BRANCH_WIDTH // FNET_GROUP_DIM)
    bf = lambda a: jnp.asarray(a, dtype=BF16)
    return dict(c1=bf(np.cos(a1)), s1=bf(np.sin(a1)), c2=bf(np.cos(a2)), s2=bf(np.sin(a2)),
                twc=jnp.asarray(np.cos(atw), F32), tws=jnp.asarray(np.sin(atw), F32),
                cc=bf(np.kron(eye, np.cos(ac))), sc=bf(np.kron(eye, np.sin(ac))))


def _fourier_mix(f, tabs):
    B, T, C = f.shape
    T1 = T // FFT_T2
    n_col = FFT_T2 * C // FFT_COLS
    full2 = lambda shape: pl.BlockSpec(shape, lambda b, i: (0, 0))
    ar, ai = pl.pallas_call(
        _fft1_kernel,
        grid=(B, n_col),
        in_specs=[pl.BlockSpec((1, T1, FFT_COLS), lambda b, i: (b, 0, i)),
                  full2((T1, T1)), full2((T1, T1))],
        out_specs=[pl.BlockSpec((1, T1, FFT_COLS), lambda b, i: (b, 0, i))] * 2,
        out_shape=[jax.ShapeDtypeStruct((B, T1, FFT_T2 * C), F32)] * 2,
        compiler_params=_cparams(("arbitrary", "arbitrary")),
    )(f.reshape(B, T1, FFT_T2 * C), tabs["c1"], tabs["s1"])
    ar = ar.reshape(B, T, C)
    ai = ai.reshape(B, T, C)
    rows = FFT_K1_BLOCK * FFT_T2
    y = pl.pallas_call(
        functools.partial(_fft2_kernel, norm=1.0 / math.sqrt(T * FNET_GROUP_DIM)),
        grid=(B, T1 // FFT_K1_BLOCK),
        in_specs=[pl.BlockSpec((1, rows, C), lambda b, i: (b, i, 0)),
                  pl.BlockSpec((1, rows, C), lambda b, i: (b, i, 0)),
                  pl.BlockSpec((rows, LANES), lambda b, i: (i, 0)),
                  pl.BlockSpec((rows, LANES), lambda b, i: (i, 0)),
                  full2((FFT_T2, FFT_T2)), full2((FFT_T2, FFT_T2)),
                  full2((C, C)), full2((C, C))],
        out_specs=pl.BlockSpec((1, FFT_T2, FFT_K1_BLOCK * C), lambda b, i: (b, 0, i)),
        out_shape=jax.ShapeDtypeStruct((B, FFT_T2, T1 * C), BF16),
        compiler_params=_cparams(("arbitrary", "arbitrary")),
    )(ar, ai, tabs["twc"], tabs["tws"], tabs["c2"], tabs["s2"], tabs["cc"], tabs["sc"])
    return y.reshape(B, T, C)


NAT_Q_ROWS = 8
NAT_Q_TOK = NAT_Q_ROWS * GRID_W
NAT_EDGE_TOK = (NAT_WIN_H // 2) * GRID_W
NAT_WIN_TOK = NAT_Q_TOK + 2 * NAT_EDGE_TOK


def _head_masks():
    lane = lax.broadcasted_iota(jnp.int32, (1, LANES), 1)
    return lane < HEAD_DIM


def _nat_kernel(q_ref, kp_ref, kc_ref, kn_ref, vp_ref, vc_ref, vn_ref, bias_ref, o_ref, kw, vw):
    kw[0:NAT_EDGE_TOK, :] = kp_ref[0]
    kw[NAT_EDGE_TOK:NAT_EDGE_TOK + NAT_Q_TOK, :] = kc_ref[0]
    kw[NAT_EDGE_TOK + NAT_Q_TOK:, :] = kn_ref[0]
    vw[0:NAT_EDGE_TOK, :] = vp_ref[0]
    vw[NAT_EDGE_TOK:NAT_EDGE_TOK + NAT_Q_TOK, :] = vc_ref[0]
    vw[NAT_EDGE_TOK + NAT_Q_TOK:, :] = vn_ref[0]
    first = _head_masks()
    ones = jnp.ones((NAT_WIN_TOK, LANES), BF16)
    for hp in range(NAT_HEADS // 2):
        cols = slice(hp * LANES, (hp + 1) * LANES)
        q = q_ref[0, :, cols]
        k = kw[:, cols]
        v1 = jnp.concatenate([vw[:, cols], ones], axis=1)
        outs = []
        for hh in range(2):
            sel = first if hh == 0 else jnp.logical_not(first)
            qm = jnp.where(sel, q, jnp.zeros_like(q))
            s = _dot_nt(qm, k) + bias_ref[0, 2 * hp + hh]
            m = jnp.max(s, axis=-1, keepdims=True)
            ov = _dot(jnp.exp2(s - m).astype(BF16), v1)
            outs.append(ov[:, :LANES] / ov[:, LANES:])
        o_ref[0, :, cols] = jnp.where(first, outs[0], outs[1]).astype(BF16)


def _nat_bias_table(rel_bias):
    H = rel_bias.shape[0]
    half = NAT_WIN_H // 2
    col = np.arange(GRID_W)
    col_start = np.clip(col - NAT_WIN_W // 2, 0, GRID_W - NAT_WIN_W)
    kc = np.arange(GRID_W)
    col_ok = (kc[None, :] >= col_start[:, None]) & (kc[None, :] < col_start[:, None] + NAT_WIN_W)
    col_off = np.clip(kc[None, :] - col[:, None] + (NAT_WIN_W - 1), 0, 2 * NAT_WIN_W - 2)
    rr = np.arange(NAT_Q_ROWS)
    wr = np.arange(NAT_Q_ROWS + NAT_WIN_H)
    start = np.stack([np.maximum(rr - half, 0) + half, rr, np.minimum(rr, half)])
    row_ok = (wr[None, None, :] >= start[:, :, None]) & (wr[None, None, :] < start[:, :, None] + NAT_WIN_H)
    row_off = np.clip(wr[None, :] - half - rr[:, None] + (NAT_WIN_H - 1), 0, 2 * NAT_WIN_H - 2)
    tab = rel_bias[:, row_off][:, :, :, col_off]
    ok = row_ok[:, None, :, :, None, None] & col_ok[None, None, None, None]
    tab = jnp.where(jnp.asarray(ok), tab.astype(F32)[None] * LOG2E, NEG_INF)
    tab = jnp.transpose(tab, (0, 1, 2, 4, 3, 5))
    return tab.reshape(3, H, NAT_Q_TOK, NAT_WIN_TOK)


def _neighborhood_attention(nq, rest, bias_tab):
    B, T, C = nq.shape
    n_blk = T // NAT_Q_TOK
    kind = lambda i: 1 - (i == 0).astype(jnp.int32) + (i == n_blk - 1).astype(jnp.int32)
    per = NAT_Q_TOK // NAT_EDGE_TOK
    n_edge = T // NAT_EDGE_TOK
    prev = lambda c: (lambda b, i: (b, jnp.maximum(i * per - 1, 0), c))
    cur = lambda c: (lambda b, i: (b, i, c))
    nxt = lambda c: (lambda b, i: (b, jnp.minimum((i + 1) * per, n_edge - 1), c))
    edge = lambda f: pl.BlockSpec((1, NAT_EDGE_TOK, C), f)
    mid = lambda f: pl.BlockSpec((1, NAT_Q_TOK, C), f)
    return pl.pallas_call(
        _nat_kernel,
        grid=(B, n_blk),
        in_specs=[mid(cur(0)),
                  edge(prev(0)), mid(cur(0)), edge(nxt(0)),
                  edge(prev(1)), mid(cur(1)), edge(nxt(1)),
                  pl.BlockSpec((1,) + bias_tab.shape[1:], lambda b, i: (kind(i), 0, 0, 0),
                               pipeline_mode=pl.Buffered(1))],
        out_specs=mid(cur(0)),
        out_shape=jax.ShapeDtypeStruct((B, T, C), BF16),
        scratch_shapes=[pltpu.VMEM((NAT_WIN_TOK, C), BF16), pltpu.VMEM((NAT_WIN_TOK, C), BF16)],
        compiler_params=_cparams(("arbitrary", "arbitrary")),
    )(nq, rest, rest, rest, rest, rest, rest, bias_tab)


DIFF_TQ = 512
DIFF_TK = 512
DIFF_UNROLL = 8
DIFF_HEADROOM = 64.0


def _qk_head_mask():
    lane = lax.broadcasted_iota(jnp.int32, (1, LANES), 1)
    return (lane % (LANES // 2)) < HEAD_DIM // 2


def _diff_kernel(q_ref, k_ref, v_ref, lq1_ref, lk1_ref, lq2_ref, lk2_ref, g_ref, o_ref, a1_scr, a2_scr,
                 *, lam_init, n_kv):
    first = _qk_head_mask()
    q = q_ref[0]
    zero = jnp.zeros_like(q)
    q1 = jnp.where(first, q, zero)
    q2 = jnp.where(first, zero, q)
    tq = q.shape[0]
    ones = jnp.ones((DIFF_TK, LANES), BF16)
    zacc = jnp.zeros((tq, 2 * LANES), F32)

    def tile(c):
        k0 = pl.multiple_of(c * DIFF_TK, DIFF_TK)
        k = k_ref[0, pl.ds(k0, DIFF_TK), :]
        v1 = jnp.concatenate([v_ref[0, pl.ds(k0, DIFF_TK), :], ones], axis=1)
        return k, v1

    k, _ = tile(0)
    r1 = jnp.max(_dot_nt(q1, k), axis=-1, keepdims=True)
    r2 = jnp.max(_dot_nt(q2, k), axis=-1, keepdims=True)

    def fast(c, carry):
        a1, a2, t1, t2 = carry
        k, v1 = tile(c)

        def one(qm, r, a, t):
            s = _dot_nt(qm, k)
            for u in range(DIFF_TK // LANES):
                t = jnp.maximum(t, s[:, u * LANES:(u + 1) * LANES])
            return a + _dot(jnp.exp2(s - r).astype(BF16), v1), t

        a1, t1 = one(q1, r1, a1, t1)
        a2, t2 = one(q2, r2, a2, t2)
        return a1, a2, t1, t2

    tneg = jnp.full((tq, LANES), NEG_INF, F32)
    a1, a2, t1, t2 = lax.fori_loop(0, n_kv, fast, (zacc, zacc, tneg, tneg), unroll=DIFF_UNROLL)
    a1_scr[...] = a1
    a2_scr[...] = a2
    growth = jnp.max(jnp.maximum(t1 - r1, t2 - r2))

    @pl.when(jnp.logical_not(growth <= DIFF_HEADROOM))
    def _():
        def slow(c, carry):
            m1, b1, m2, b2 = carry
            k, v1 = tile(c)

            def one(qm, m, a):
                s = _dot_nt(qm, k)
                mn = jnp.maximum(m, jnp.max(s, axis=-1, keepdims=True))
                return mn, jnp.exp2(m - mn) * a + _dot(jnp.exp2(s - mn).astype(BF16), v1)

            m1, b1 = one(q1, m1, b1)
            m2, b2 = one(q2, m2, b2)
            return m1, b1, m2, b2

        neg = jnp.full((tq, 1), NEG_INF, F32)
        _, b1, _, b2 = lax.fori_loop(0, n_kv, slow, (neg, zacc, neg, zacc))
        a1_scr[...] = b1
        a2_scr[...] = b2

    lam = (jnp.exp(jnp.sum(lq1_ref[...] * lk1_ref[...], keepdims=True))
           - jnp.exp(jnp.sum(lq2_ref[...] * lk2_ref[...], keepdims=True)) + lam_init)
    o = (a1_scr[:, :LANES] / a1_scr[:, LANES:]) - lam * (a2_scr[:, :LANES] / a2_scr[:, LANES:])
    ms = jnp.mean(o * o, axis=-1, keepdims=True)
    o = o * lax.rsqrt(ms + EPS) * g_ref[...]
    o_ref[0] = (o * (1.0 - lam_init)).astype(BF16)


def _diff_attention(tok, rest, lq1, lk1, lq2, lk2, ln_g, lam_init):
    B, T, _ = tok.shape
    tq = min(DIFF_TQ, T)
    k_blk = 1024 // LANES
    v_blk = 1024 // LANES
    vec = lambda n: pl.BlockSpec((1, n), lambda b, h, i: (0, 0))
    return pl.pallas_call(
        functools.partial(_diff_kernel, lam_init=lam_init, n_kv=T // DIFF_TK),
        grid=(B, DIFF_HEADS, T // tq),
        in_specs=[pl.BlockSpec((1, tq, LANES), lambda b, h, i: (b, i, h)),
                  pl.BlockSpec((1, T, LANES), lambda b, h, i: (b, 0, k_blk + h)),
                  pl.BlockSpec((1, T, LANES), lambda b, h, i: (b, 0, v_blk + h)),
                  vec(HEAD_DIM), vec(HEAD_DIM), vec(HEAD_DIM), vec(HEAD_DIM), vec(LANES)],
        out_specs=pl.BlockSpec((1, tq, LANES), lambda b, h, i: (b, i, h)),
        out_shape=jax.ShapeDtypeStruct((B, T, DIFF_HEADS * LANES), BF16),
        scratch_shapes=[pltpu.VMEM((tq, 2 * LANES), F32), pltpu.VMEM((tq, 2 * LANES), F32)],
        compiler_params=_cparams(("arbitrary", "arbitrary", "arbitrary")),
    )(tok, tok, rest, lq1, lk1, lq2, lk2, ln_g)


DIL_TQ = 512
DIL_EDGE = 128


def _dil_kernel(q_ref, kp_ref, kc_ref, kn_ref, vp_ref, vc_ref, vn_ref, o_ref, lse_ref, kw, vw, *, seq, tq):
    i = pl.program_id(2)
    kw[0:DIL_EDGE, :] = kp_ref[0]
    kw[DIL_EDGE:DIL_EDGE + tq, :] = kc_ref[0]
    kw[DIL_EDGE + tq:, :] = kn_ref[0]
    vw[0:DIL_EDGE, :] = vp_ref[0]
    vw[DIL_EDGE:DIL_EDGE + tq, :] = vc_ref[0]
    vw[DIL_EDGE + tq:, :] = vn_ref[0]
    win = tq + 2 * DIL_EDGE
    s0 = i * tq
    qpos = s0 + lax.broadcasted_iota(jnp.int32, (tq, win), 0)
    kpos = s0 - DIL_EDGE + lax.broadcasted_iota(jnp.int32, (tq, win), 1)
    valid = (jnp.abs(kpos - qpos) <= DIL_SIDE) & (kpos >= 0) & (kpos < seq)
    first_qk = _qk_head_mask()
    first_v = _head_masks()
    ones = jnp.ones((win, LANES), BF16)
    for hp in range(DIL_HEADS // 2):
        cols = slice(hp * LANES, (hp + 1) * LANES)
        q = q_ref[0, :, cols]
        k = kw[:, cols]
        v1 = jnp.concatenate([vw[:, cols], ones], axis=1)
        outs, lses = [], []
        for hh in range(2):
            sel = first_qk if hh == 0 else jnp.logical_not(first_qk)
            qm = jnp.where(sel, q, jnp.zeros_like(q))
            s = jnp.where(valid, _dot_nt(qm, k), NEG_INF)
            m = jnp.max(s, axis=-1, keepdims=True)
            ov = _dot(jnp.exp2(s - m).astype(BF16), v1)
            l = ov[:, LANES:]
            outs.append(ov[:, :LANES] / l)
            lses.append(m + jnp.log2(l))
        o_ref[0, :, cols] = jnp.where(first_v, outs[0], outs[1]).astype(BF16)
        lse_ref[0, :, cols] = jnp.where(first_v, lses[0], lses[1])


def _dilated_group(qa, ka, va, qc, kc, vc, dil):
    B, seq, _ = qa.shape
    C = DIL_HEADS * HEAD_DIM
    tq = min(DIL_TQ, seq)
    per = tq // DIL_EDGE
    n_edge = seq // DIL_EDGE
    prev = lambda cf: (lambda b, r, i: (b, jnp.maximum(i * per - 1, 0), cf(r)))
    cur = lambda cf: (lambda b, r, i: (b, i, cf(r)))
    nxt = lambda cf: (lambda b, r, i: (b, jnp.minimum((i + 1) * per, n_edge - 1), cf(r)))
    edge = lambda f: pl.BlockSpec((1, DIL_EDGE, C), f)
    mid = lambda f: pl.BlockSpec((1, tq, C), f)
    out_map = lambda b, r, i: (b, i, r)
    return pl.pallas_call(
        functools.partial(_dil_kernel, seq=seq, tq=tq),
        grid=(B, dil, seq // tq),
        in_specs=[mid(cur(qc)),
                  edge(prev(kc)), mid(cur(kc)), edge(nxt(kc)),
                  edge(prev(vc)), mid(cur(vc)), edge(nxt(vc))],
        out_specs=[pl.BlockSpec((1, tq, C), out_map), pl.BlockSpec((1, tq, C), out_map)],
        out_shape=[jax.ShapeDtypeStruct((B, seq, dil * C), BF16),
                   jax.ShapeDtypeStruct((B, seq, dil * C), F32)],
        scratch_shapes=[pltpu.VMEM((tq + 2 * DIL_EDGE, C), BF16), pltpu.VMEM((tq + 2 * DIL_EDGE, C), BF16)],
        compiler_params=_cparams(("arbitrary", "arbitrary", "arbitrary")),
    )(qa, ka, ka, ka, va, va, va)


MERGE_CHUNK = 512


def _merge_kernel(x_ref, gt_ref, ya_ref, yb_ref, yc_ref, o0_ref, o1_ref, o2_ref, s0_ref, s1_ref, s2_ref,
                  g0_ref, g1_ref, g2_ref, g3_ref, wb_ref, wo_ref, out_ref,
                  yd_scr, acc_scr, o1_scr, s1_scr, o2_scr, s2_scr):
    j = pl.program_id(2)
    tm = acc_scr.shape[0]
    C = BRANCH_WIDTH

    @pl.when(j == 0)
    def _():
        for d, o_ref, s_ref, o_scr, s_scr in ((4, o1_ref, s1_ref, o1_scr, s1_scr),
                                              (16, o2_ref, s2_ref, o2_scr, s2_scr)):
            for r in range(d):
                for u in range(C // LANES):
                    cols = slice(r * C + u * LANES, r * C + (u + 1) * LANES)
                    o_scr[u, pl.ds(r, tm // d, stride=d), :] = o_ref[0, :, cols].astype(F32)
                    s_scr[u, pl.ds(r, tm // d, stride=d), :] = s_ref[0, :, cols]
        wide = lambda scr: jnp.concatenate([scr[u] for u in range(C // LANES)], axis=1)
        s0 = s0_ref[0]
        s1 = wide(s1_scr)
        s2 = wide(s2_scr)
        m = jnp.maximum(jnp.maximum(s0, s1), s2)
        e0 = jnp.exp2(s0 - m)
        e1 = jnp.exp2(s1 - m)
        e2 = jnp.exp2(s2 - m)
        num = o0_ref[0].astype(F32) * e0 + wide(o1_scr) * e1 + wide(o2_scr) * e2
        yd_scr[...] = (num / (e0 + e1 + e2)).astype(BF16)
        acc_scr[...] = jnp.zeros_like(acc_scr)

    merged = g0_ref[0].astype(F32) * _dot(ya_ref[0], wb_ref[0])
    merged += g1_ref[0].astype(F32) * _dot(yb_ref[0], wb_ref[1])
    merged += g2_ref[0].astype(F32) * _dot(yc_ref[0], wb_ref[2])
    merged += g3_ref[0].astype(F32) * _dot(yd_scr[...], wb_ref[3])
    acc_scr[...] += _dot(merged.astype(BF16), wo_ref[...])

    @pl.when(j == pl.num_programs(2) - 1)
    def _():
        out_ref[0] = x_ref[0] + gt_ref[0] * acc_scr[...]


def _merge(x, gt, ya, yb, yc, dil_o, dil_lse, gates, wb, wo):
    B, T, D = x.shape
    tm = min(MERGE_TILE, T)
    C = BRANCH_WIDTH
    n_j = D // MERGE_CHUNK
    tok = pl.BlockSpec((1, tm, C), lambda b, i, j: (b, i, 0))
    res = lambda d: pl.BlockSpec((1, tm // d, d * C), lambda b, i, j: (b, i, 0))
    gate = lambda n: pl.BlockSpec((1, tm, MERGE_CHUNK), lambda b, i, j: (b, i, n * n_j + j))
    dil_specs = [res(d) for d in DIL_DILATIONS]
    return pl.pallas_call(
        _merge_kernel,
        grid=(B, T // tm, n_j),
        in_specs=[pl.BlockSpec((1, tm, D), lambda b, i, j: (b, i, 0)),
                  pl.BlockSpec((1, 1, D), lambda b, i, j: (b, 0, 0)),
                  tok, tok, tok, *dil_specs, *dil_specs,
                  gate(0), gate(1), gate(2), gate(3),
                  pl.BlockSpec((N_BRANCH, C, MERGE_CHUNK), lambda b, i, j: (0, 0, j)),
                  pl.BlockSpec((MERGE_CHUNK, D), lambda b, i, j: (j, 0))],
        out_specs=pl.BlockSpec((1, tm, D), lambda b, i, j: (b, i, 0)),
        out_shape=jax.ShapeDtypeStruct((B, T, D), F32),
        scratch_shapes=[pltpu.VMEM((tm, C), BF16), pltpu.VMEM((tm, D), F32)]
                       + [pltpu.VMEM((C // LANES, tm, LANES), F32)] * 4,
        compiler_params=_cparams(("arbitrary", "arbitrary", "arbitrary")),
    )(x, gt, ya, yb, yc, *dil_o, *dil_lse, gates, gates, gates, gates, wb, wo)


def _prep_ffn(w_in, w_out):
    D = w_in.shape[0]
    pad = D_FF_PAD - D_FF
    a = jnp.pad(w_in[:, :D_FF].astype(BF16), ((0, 0), (0, pad)))
    b = jnp.pad(w_in[:, D_FF:].astype(BF16), ((0, 0), (0, pad)))
    n = D_FF_PAD // FF_CHUNK
    wab = jnp.concatenate([a.reshape(D, n, FF_CHUNK), b.reshape(D, n, FF_CHUNK)], axis=2)
    wo = jnp.pad(w_out.astype(BF16), ((0, pad), (0, 0)))
    return wab.reshape(D, n * 2 * FF_CHUNK), wo


def _qk_lane_order():
    rot_half = ROT_DIM // 2
    rest_half = (HEAD_DIM - ROT_DIM) // 2
    idx = []
    for n in range(LANES):
        half, w = divmod(n, LANES // 2)
        head, i = divmod(w, HEAD_DIM // 2)
        d = i + rot_half * half if i < rot_half else ROT_DIM + (i - rot_half) + rest_half * half
        idx.append(head * HEAD_DIM + d)
    return np.asarray(idx)


def _prep_w_in(w):
    W = BRANCH_WIDTH
    D = w.shape[0]
    qscale = HEAD_DIM ** -0.5 * LOG2E
    order = _qk_lane_order()
    qk = lambda cols: cols.reshape(D, -1, LANES)[:, :, order].reshape(D, -1)
    f_in, nq, nk, nv, dq, dk, dv = [w[:, n * W:(n + 1) * W] for n in range(7)]
    lq = [qk(w[:, (7 + g) * W:(8 + g) * W] * qscale) for g in range(3)]
    lk = [qk(w[:, (10 + g) * W:(11 + g) * W]) for g in range(3)]
    lv = [w[:, (13 + g) * W:(14 + g) * W] for g in range(3)]
    gates = w[:, MIX_COLS:]
    cols = [qk(dq * qscale), lq[0], lq[1], lq[2], qk(dk), lk[0], lk[1], lk[2],
            f_in, nq * qscale, nk, nv, dv, lv[0], lv[1], lv[2], gates]
    return jnp.concatenate(cols, axis=1).astype(BF16)


def _rope_tables(T):
    rot_half = ROT_DIM // 2
    inv = ROPE_THETA ** (-jnp.arange(0, ROT_DIM, 2, dtype=F32) / ROT_DIM)
    ang = jnp.arange(T, dtype=F32)[:, None] * inv[None, :]
    cos, sin = jnp.cos(ang), jnp.sin(ang)
    lane = np.arange(LANES)
    i = lane % (HEAD_DIM // 2)
    is_rot = jnp.asarray(i < rot_half)[None, :]
    src = np.minimum(i, rot_half - 1)
    sign = jnp.asarray(np.where(lane < LANES // 2, -1.0, 1.0), F32)[None, :]
    ra = jnp.where(is_rot, cos[:, src], 1.0)
    rb = jnp.where(is_rot, sin[:, src] * sign, 0.0)
    return ra, rb


def _trunk(x, mods, P, tabs):
    B, T, D = x.shape
    for l in range(DEPTH):
        sh1, sc1, gt1, sh2, sc2, gt2, sh3, sc3, gt3 = [mods[l][:, i:i + 1, :] for i in range(N_MOD)]
        x = _ffn(x, P["g_ffn1"][l], sh1, sc1, gt1, P["wab1"][l], P["wo1"][l], P["g_final"], False)
        tok, d4, d16, f_in, nq, rest, gates = _inproj(x, P["g_mix"][l], sh2, sc2, P["w_in"][l],
                                                      P["b_gate"][l], *tabs["rope"])
        ya = _fourier_mix(f_in, tabs["dft"])
        yb = _neighborhood_attention(nq, rest, P["nat_bias"][l])
        lam_init = 0.8 - 0.6 * math.exp(-0.3 * l)
        yc = _diff_attention(tok, rest, P["lam_q1"][l], P["lam_k1"][l], P["lam_q2"][l], P["lam_k2"][l],
                             P["diff_ln_g"][l], lam_init)
        dil = [_dilated_group(tok, tok, rest, lambda r: 1, lambda r: 3, lambda r: 3, 1),
               _dilated_group(d4, d4, d4, lambda r: r, lambda r: 4 + r, lambda r: 8 + r, 4),
               _dilated_group(d16, d16, d16, lambda r: r, lambda r: 16 + r, lambda r: 32 + r, 16)]
        x = _merge(x, gt2, ya, yb, yc, [o for o, _ in dil], [s for _, s in dil], gates,
                   P["w_branch"][l], P["w_out"][l])
        x = _ffn(x, P["g_ffn2"][l], sh3, sc3, gt3, P["wab2"][l], P["wo2"][l], P["g_final"], l == DEPTH - 1)
    return x


def kernel(x_prompt, x_sample, c_prompt, c_sample, w_ada, b_ada, g_ffn1, w_ffn1_in, w_ffn1_out, g_mix, w_in, b_gate, nat_rel_bias, lam_q1, lam_k1, lam_q2, lam_k2, diff_ln_g, w_branch, w_out, g_ffn2, w_ffn2_in, w_ffn2_out, g_final):
    L = DEPTH
    D = D_MODEL
    Bp, Bs = c_prompt.shape[0], c_sample.shape[0]
    rows = ((Bp + Bs + 7) // 8) * 8
    c_all = jnp.concatenate([c_prompt, c_sample, jnp.zeros((rows - Bp - Bs, D), F32)], axis=0)
    mods = _modulation(c_all, w_ada, b_ada).reshape(L, rows, N_MOD, D)

    P = {}
    ffn1 = [_prep_ffn(w_ffn1_in[l], w_ffn1_out[l]) for l in range(L)]
    ffn2 = [_prep_ffn(w_ffn2_in[l], w_ffn2_out[l]) for l in range(L)]
    P["wab1"] = [a for a, _ in ffn1]
    P["wo1"] = [b for _, b in ffn1]
    P["wab2"] = [a for a, _ in ffn2]
    P["wo2"] = [b for _, b in ffn2]
    P["w_in"] = [_prep_w_in(w_in[l]) for l in range(L)]
    P["b_gate"] = [b_gate[l].reshape(1, N_BRANCH * D) for l in range(L)]
    P["nat_bias"] = [_nat_bias_table(nat_rel_bias[l]) for l in range(L)]
    P["w_branch"] = [w_branch[l].astype(BF16) for l in range(L)]
    P["w_out"] = [w_out[l].astype(BF16) for l in range(L)]
    for name, arr in (("g_ffn1", g_ffn1), ("g_mix", g_mix), ("g_ffn2", g_ffn2), ("lam_q1", lam_q1),
                      ("lam_k1", lam_k1), ("lam_q2", lam_q2), ("lam_k2", lam_k2), ("diff_ln_g", diff_ln_g)):
        P[name] = [arr[l].reshape(1, -1) for l in range(L)]
    P["g_final"] = g_final.reshape(1, D)

    outs = []
    for x, lo, hi in ((x_prompt, 0, Bp), (x_sample, Bp, Bp + Bs)):
        T = x.shape[1]
        tabs = {"rope": _rope_tables(T), "dft": _dft_tables(T)}
        outs.append(_trunk(x, mods[:, lo:hi], P, tabs))
    return tuple(outs)
```

```python
import functools
import math

import numpy as np
import jax
import jax.numpy as jnp
from jax import lax
from jax.experimental import pallas as pl
from jax.experimental.pallas import tpu as pltpu

F32 = jnp.float32
BF16 = jnp.bfloat16

D_MODEL = 2048
DEPTH = 2
GRID_W = 64
HEAD_DIM = 64
ROT_DIM = HEAD_DIM // 4
ROPE_THETA = 500000.0
EPS = 1e-6
HALF = 0.5
FNET_GROUP_DIM = 128
NAT_HEADS = 8
NAT_WIN_H = 8
NAT_WIN_W = 16
DIFF_HEADS = 4
DIL_DILATIONS = (1, 4, 16)
DIL_SIDE = 64
DIL_HEADS = 8
N_BRANCH = 4
BRANCH_WIDTH = 512
N_MOD = 9
D_FF = ((8 * D_MODEL // 3 + 63) // 64) * 64
MIX_COLS = 8192

LANES = 128
VMEM_LIMIT = 56 * 1024 * 1024
FF_CHUNK = 512
D_FF_PAD = ((D_FF + FF_CHUNK - 1) // FF_CHUNK) * FF_CHUNK
FFN_TILE = 512
TOKEN_TILE = 1024
MERGE_TILE = 256
PROJ_CHUNK = 1024
MXU_COLS = 256
NEG_INF = -1e30
LOG2E = math.log2(math.e)


def _cparams(sem):
    return pltpu.CompilerParams(dimension_semantics=sem, vmem_limit_bytes=VMEM_LIMIT)


def _dot(a, b):
    return jnp.dot(a, b, preferred_element_type=F32)


def _dot_nt(a, b):
    return lax.dot_general(a, b, (((1,), (1,)), ((), ())), preferred_element_type=F32)


def _norm_modulate(x, g, sh, sc):
    ms = jnp.mean(x * x, axis=-1, keepdims=True)
    y = x * lax.rsqrt(ms + EPS) * g
    return y * (1.0 + sc) + sh


def _mod_kernel(c_ref, w_ref, b_ref, o_ref):
    c = c_ref[...]
    s = c * jax.nn.sigmoid(c)
    o_ref[0] = _dot(s.astype(BF16), w_ref[0].astype(BF16)) + b_ref[0]


def _modulation(c_all, w_ada, b_ada):
    L, D, N = w_ada.shape
    R = c_all.shape[0]
    tn = 1152
    return pl.pallas_call(
        _mod_kernel,
        grid=(L, N // tn),
        in_specs=[
            pl.BlockSpec((R, D), lambda l, j: (0, 0)),
            pl.BlockSpec((1, D, tn), lambda l, j: (l, 0, j)),
            pl.BlockSpec((1, 1, tn), lambda l, j: (l, 0, j)),
        ],
        out_specs=pl.BlockSpec((1, R, tn), lambda l, j: (l, 0, j)),
        out_shape=jax.ShapeDtypeStruct((L, R, N), F32),
        compiler_params=_cparams(("arbitrary", "arbitrary")),
    )(c_all, w_ada, b_ada.reshape(L, 1, N))


def _ffn_kernel(x_ref, g_ref, sh_ref, sc_ref, gt_ref, wab_ref, wo_ref, gf_ref, o_ref, h_scr, acc_scr,
                *, final_norm):
    k = pl.program_id(2)

    @pl.when(k == 0)
    def _():
        h = _norm_modulate(x_ref[0], g_ref[...], sh_ref[0], sc_ref[0])
        h_scr[...] = h.astype(BF16)
        acc_scr[...] = jnp.zeros_like(acc_scr)

    ab = _dot(h_scr[...], wab_ref[...])
    a = ab[:, :FF_CHUNK]
    b = ab[:, FF_CHUNK:]
    act = (a * jax.nn.sigmoid(a)) * b
    acc_scr[...] += _dot(act.astype(BF16), wo_ref[...])

    @pl.when(k == pl.num_programs(2) - 1)
    def _():
        xn = x_ref[0] + (HALF * gt_ref[0]) * acc_scr[...]
        if final_norm:
            ms = jnp.mean(xn * xn, axis=-1, keepdims=True)
            xn = xn * lax.rsqrt(ms + EPS) * gf_ref[...]
        o_ref[0] = xn


def _ffn(x, g, sh, sc, gt, wab, wo, g_final, final_norm):
    B, T, D = x.shape
    tm = min(FFN_TILE, T)
    nk = wo.shape[0] // FF_CHUNK
    vec = pl.BlockSpec((1, D), lambda b, i, k: (0, 0))
    per_b = pl.BlockSpec((1, 1, D), lambda b, i, k: (b, 0, 0))
    return pl.pallas_call(
        functools.partial(_ffn_kernel, final_norm=final_norm),
        grid=(B, T // tm, nk),
        in_specs=[
            pl.BlockSpec((1, tm, D), lambda b, i, k: (b, i, 0)),
            vec, per_b, per_b, per_b,
            pl.BlockSpec((D, 2 * FF_CHUNK), lambda b, i, k: (0, k)),
            pl.BlockSpec((FF_CHUNK, D), lambda b, i, k: (k, 0)),
            vec,
        ],
        out_specs=pl.BlockSpec((1, tm, D), lambda b, i, k: (b, i, 0)),
        out_shape=jax.ShapeDtypeStruct((B, T, D), F32),
        scratch_shapes=[pltpu.VMEM((tm, D), BF16), pltpu.VMEM((tm, D), F32)],
        compiler_params=_cparams(("arbitrary", "arbitrary", "arbitrary")),
    )(x, g, sh, sc, gt, wab, wo, g_final)


GATE_STEPS = N_BRANCH * D_MODEL // PROJ_CHUNK
FIRST_GATE_STEP = 8
DIL_SECTION = {d: d * BRANCH_WIDTH for d in DIL_DILATIONS}


def _rope_apply(z, ra, rb):
    return z * ra + pltpu.roll(z, LANES // 2, 1) * rb


def _inproj_kernel(x_ref, g_ref, sh_ref, sc_ref, w_ref, bg_ref, ra_ref, rb_ref,
                   tok_o, d4_o, d16_o, f_o, nq_o, rest_o, gate_o, h_scr, z_scr):
    j = pl.program_id(2)
    tm = h_scr.shape[0]
    n_sub = PROJ_CHUNK // MXU_COLS

    @pl.when(j == 0)
    def _():
        h = _norm_modulate(x_ref[0], g_ref[...], sh_ref[0], sc_ref[0])
        h_scr[...] = h.astype(BF16)

    def zsub(c):
        return _dot(h_scr[...], w_ref[:, c * MXU_COLS:(c + 1) * MXU_COLS])

    def roped(c):
        z = zsub(c)
        ra = ra_ref[...]
        rb = rb_ref[...]
        return [_rope_apply(z[:, u * LANES:(u + 1) * LANES], ra, rb) for u in range(MXU_COLS // LANES)]

    sub_groups = MXU_COLS // LANES
    dil_groups = BRANCH_WIDTH // LANES

    def scatter_residues():
        for d, o_ref, g0 in ((4, d4_o, 0), (16, d16_o, dil_groups)):
            for r in range(d):
                for u in range(dil_groups):
                    rows = z_scr[g0 + u, pl.ds(r, tm // d, stride=d), :]
                    lo = r * BRANCH_WIDTH + u * LANES
                    o_ref[0, :, lo:lo + LANES] = rows.astype(BF16)

    @pl.when((j == 0) | (j == 2))
    def _():
        for c in range(n_sub):
            for u, r in enumerate(roped(c)):
                lo = c * MXU_COLS + u * LANES
                tok_o[0, :, lo:lo + LANES] = r.astype(BF16)

    @pl.when((j == 1) | (j == 3))
    def _():
        for c in range(n_sub):
            for u, r in enumerate(roped(c)):
                z_scr[c * sub_groups + u] = r
        scatter_residues()

    @pl.when(j == 7)
    def _():
        for c in range(n_sub):
            z = zsub(c)
            for u in range(sub_groups):
                z_scr[c * sub_groups + u] = z[:, u * LANES:(u + 1) * LANES]
        scatter_residues()

    @pl.when(j == 4)
    def _():
        for c in range(n_sub):
            o_ref, lo = (f_o, c * MXU_COLS) if c < n_sub // 2 else (nq_o, (c - n_sub // 2) * MXU_COLS)
            o_ref[0, :, lo:lo + MXU_COLS] = zsub(c).astype(BF16)

    @pl.when((j == 5) | (j == 6))
    def _():
        for c in range(n_sub):
            rest_o[0, :, c * MXU_COLS:(c + 1) * MXU_COLS] = zsub(c).astype(BF16)

    @pl.when(j >= FIRST_GATE_STEP)
    def _():
        for c in range(n_sub):
            cols = slice(c * MXU_COLS, (c + 1) * MXU_COLS)
            pre = zsub(c) + bg_ref[:, cols]
            gate_o[0, :, cols] = (0.5 * jnp.tanh(0.5 * pre) + 0.5).astype(BF16)


def _inproj(x, g, sh, sc, w, bg, ra, rb):
    B, T, D = x.shape
    tm = min(TOKEN_TILE, T)
    n_steps = w.shape[1] // PROJ_CHUNK
    vec = pl.BlockSpec((1, D), lambda b, i, j: (0, 0))
    per_b = pl.BlockSpec((1, 1, D), lambda b, i, j: (b, 0, 0))
    tab = pl.BlockSpec((tm, LANES), lambda b, i, j: (i, 0))
    step = lambda j, *edges: sum((j >= e).astype(jnp.int32) for e in edges)
    dil_spec = lambda d: pl.BlockSpec((1, tm // d, DIL_SECTION[d]), lambda b, i, j: (b, i, step(j, 3, 7)))
    tok512 = pl.BlockSpec((1, tm, BRANCH_WIDTH), lambda b, i, j: (b, i, 0))
    return pl.pallas_call(
        _inproj_kernel,
        grid=(B, T // tm, n_steps),
        in_specs=[
            pl.BlockSpec((1, tm, D), lambda b, i, j: (b, i, 0), pipeline_mode=pl.Buffered(1)),
            vec, per_b, per_b,
            pl.BlockSpec((D, PROJ_CHUNK), lambda b, i, j: (0, j)),
            pl.BlockSpec((1, PROJ_CHUNK), lambda b, i, j: (0, jnp.maximum(j - FIRST_GATE_STEP, 0))),
            tab, tab,
        ],
        out_specs=[
            pl.BlockSpec((1, tm, PROJ_CHUNK), lambda b, i, j: (b, i, step(j, 2))),
            dil_spec(4), dil_spec(16),
            tok512, tok512,
            pl.BlockSpec((1, tm, PROJ_CHUNK), lambda b, i, j: (b, i, step(j, 6))),
            pl.BlockSpec((1, tm, PROJ_CHUNK), lambda b, i, j: (b, i, jnp.maximum(j - FIRST_GATE_STEP, 0))),
        ],
        out_shape=[
            jax.ShapeDtypeStruct((B, T, 2 * PROJ_CHUNK), BF16),
            jax.ShapeDtypeStruct((B, T // 4, 3 * DIL_SECTION[4]), BF16),
            jax.ShapeDtypeStruct((B, T // 16, 3 * DIL_SECTION[16]), BF16),
            jax.ShapeDtypeStruct((B, T, BRANCH_WIDTH), BF16),
            jax.ShapeDtypeStruct((B, T, BRANCH_WIDTH), BF16),
            jax.ShapeDtypeStruct((B, T, 2 * PROJ_CHUNK), BF16),
            jax.ShapeDtypeStruct((B, T, GATE_STEPS * PROJ_CHUNK), BF16),
        ],
        scratch_shapes=[pltpu.VMEM((tm, D), BF16), pltpu.VMEM((PROJ_CHUNK // LANES, tm, LANES), F32)],
        compiler_params=_cparams(("arbitrary", "arbitrary", "arbitrary")),
    )(x, g, sh, sc, w, bg, ra, rb)


FFT_T2 = 128
FFT_COLS = 2048
FFT_K1_BLOCK = 4


def _fft1_kernel(u_ref, c1_ref, s1_ref, ar_ref, ai_ref):
    u = u_ref[0]
    ar_ref[0] = _dot(c1_ref[...], u).astype(BF16)
    ai_ref[0] = (-_dot(s1_ref[...], u)).astype(BF16)


def _fft2_kernel(ar_ref, ai_ref, twc_ref, tws_ref, c2_ref, s2_ref, cc_ref, sc_ref, o_ref, *, norm):
    c2 = c2_ref[...]
    s2 = s2_ref[...]
    cc = cc_ref[...]
    sc = sc_ref[...]
    for kk in range(FFT_K1_BLOCK):
        rows = slice(kk * FFT_T2, (kk + 1) * FFT_T2)
        ar = ar_ref[0, rows, :].astype(F32)
        ai = ai_ref[0, rows, :].astype(F32)
        twc = jnp.concatenate([twc_ref[rows, :]] * (BRANCH_WIDTH // LANES), axis=1)
        tws = jnp.concatenate([tws_ref[rows, :]] * (BRANCH_WIDTH // LANES), axis=1)
        br = (ar * twc + ai * tws).astype(BF16)
        bi = (ai * twc - ar * tws).astype(BF16)
        zr = _dot(c2, br) + _dot(s2, bi)
        zi = _dot(c2, bi) - _dot(s2, br)
        y = _dot(zr.astype(BF16), cc) + _dot(zi.astype(BF16), sc)
        o_ref[0, :, kk * BRANCH_WIDTH:(kk + 1) * BRANCH_WIDTH] = (y * norm).astype(BF16)


def _dft_tables(T):
    T1 = T // FFT_T2
    k1 = np.arange(T1)
    a1 = 2.0 * np.pi * np.outer(k1, k1) / T1
    k2 = np.arange(FFT_T2)
    a2 = 2.0 * np.pi * np.outer(k2, k2) / FFT_T2
    atw = 2.0 * np.pi * np.outer(k1, k2).reshape(T, 1) / T
    atw = np.broadcast_to(atw, (T, LANES))
    ch = np.arange(FNET_GROUP_DIM)
    ac = 2.0 * np.pi * np.outer(ch, ch) / FNET_GROUP_DIM
    eye = np.eye(BRANCH_WIDTH // FNET_GROUP_DIM)
    bf = lambda a: jnp.asarray(a, dtype=BF16)
    return dict(c1=bf(np.cos(a1)), s1=bf(np.sin(a1)), c2=bf(np.cos(a2)), s2=bf(np.sin(a2)),
                twc=jnp.asarray(np.cos(atw), F32), tws=jnp.asarray(np.sin(atw), F32),
                cc=bf(np.kron(eye, np.cos(ac))), sc=bf(np.kron(eye, np.sin(ac))))


def _fourier_mix(f, tabs):
    B, T, C = f.shape
    T1 = T // FFT_T2
    n_col = FFT_T2 * C // FFT_COLS
    full2 = lambda shape: pl.BlockSpec(shape, lambda b, i: (0, 0))
    ar, ai = pl.pallas_call(
        _fft1_kernel,
        grid=(B, n_col),
        in_specs=[pl.BlockSpec((1, T1, FFT_COLS), lambda b, i: (b, 0, i)),
                  full2((T1, T1)), full2((T1, T1))],
        out_specs=[pl.BlockSpec((1, T1, FFT_COLS), lambda b, i: (b, 0, i))] * 2,
        out_shape=[jax.ShapeDtypeStruct((B, T1, FFT_T2 * C), BF16)] * 2,
        compiler_params=_cparams(("arbitrary", "arbitrary")),
    )(f.reshape(B, T1, FFT_T2 * C), tabs["c1"], tabs["s1"])
    ar = ar.reshape(B, T, C)
    ai = ai.reshape(B, T, C)
    rows = FFT_K1_BLOCK * FFT_T2
    y = pl.pallas_call(
        functools.partial(_fft2_kernel, norm=1.0 / math.sqrt(T * FNET_GROUP_DIM)),
        grid=(B, T1 // FFT_K1_BLOCK),
        in_specs=[pl.BlockSpec((1, rows, C), lambda b, i: (b, i, 0)),
                  pl.BlockSpec((1, rows, C), lambda b, i: (b, i, 0)),
                  pl.BlockSpec((rows, LANES), lambda b, i: (i, 0)),
                  pl.BlockSpec((rows, LANES), lambda b, i: (i, 0)),
                  full2((FFT_T2, FFT_T2)), full2((FFT_T2, FFT_T2)),
                  full2((C, C)), full2((C, C))],
        out_specs=pl.BlockSpec((1, FFT_T2, FFT_K1_BLOCK * C), lambda b, i: (b, 0, i)),
        out_shape=jax.ShapeDtypeStruct((B, FFT_T2, T1 * C), BF16),
        compiler_params=_cparams(("arbitrary", "arbitrary")),
    )(ar, ai, tabs["twc"], tabs["tws"], tabs["c2"], tabs["s2"], tabs["cc"], tabs["sc"])
    return y.reshape(B, T, C)


NAT_Q_ROWS = 8
NAT_Q_TOK = NAT_Q_ROWS * GRID_W
NAT_EDGE_TOK = (NAT_WIN_H // 2) * GRID_W
NAT_WIN_TOK = NAT_Q_TOK + 2 * NAT_EDGE_TOK


def _head_masks():
    lane = lax.broadcasted_iota(jnp.int32, (1, LANES), 1)
    return lane < HEAD_DIM


def _nat_kernel(q_ref, kp_ref, kc_ref, kn_ref, vp_ref, vc_ref, vn_ref, bias_ref, o_ref, kw, vw):
    kw[0:NAT_EDGE_TOK, :] = kp_ref[0]
    kw[NAT_EDGE_TOK:NAT_EDGE_TOK + NAT_Q_TOK, :] = kc_ref[0]
    kw[NAT_EDGE_TOK + NAT_Q_TOK:, :] = kn_ref[0]
    vw[0:NAT_EDGE_TOK, :] = vp_ref[0]
    vw[NAT_EDGE_TOK:NAT_EDGE_TOK + NAT_Q_TOK, :] = vc_ref[0]
    vw[NAT_EDGE_TOK + NAT_Q_TOK:, :] = vn_ref[0]
    first = _head_masks()
    ones = jnp.ones((NAT_WIN_TOK, LANES), BF16)
    for hp in range(NAT_HEADS // 2):
        cols = slice(hp * LANES, (hp + 1) * LANES)
        q = q_ref[0, :, cols]
        k = kw[:, cols]
        v1 = jnp.concatenate([vw[:, cols], ones], axis=1)
        outs = []
        for hh in range(2):
            sel = first if hh == 0 else jnp.logical_not(first)
            qm = jnp.where(sel, q, jnp.zeros_like(q))
            s = _dot_nt(qm, k) + bias_ref[0, 2 * hp + hh]
            m = jnp.max(s, axis=-1, keepdims=True)
            ov = _dot(jnp.exp2(s - m).astype(BF16), v1)
            outs.append(ov[:, :LANES] / ov[:, LANES:])
        o_ref[0, :, cols] = jnp.where(first, outs[0], outs[1]).astype(BF16)


def _nat_bias_table(rel_bias):
    H = rel_bias.shape[0]
    half = NAT_WIN_H // 2
    col = np.arange(GRID_W)
    col_start = np.clip(col - NAT_WIN_W // 2, 0, GRID_W - NAT_WIN_W)
    kc = np.arange(GRID_W)
    col_ok = (kc[None, :] >= col_start[:, None]) & (kc[None, :] < col_start[:, None] + NAT_WIN_W)
    col_off = np.clip(kc[None, :] - col[:, None] + (NAT_WIN_W - 1), 0, 2 * NAT_WIN_W - 2)
    rr = np.arange(NAT_Q_ROWS)
    wr = np.arange(NAT_Q_ROWS + NAT_WIN_H)
    start = np.stack([np.maximum(rr - half, 0) + half, rr, np.minimum(rr, half)])
    row_ok = (wr[None, None, :] >= start[:, :, None]) & (wr[None, None, :] < start[:, :, None] + NAT_WIN_H)
    row_off = np.clip(wr[None, :] - half - rr[:, None] + (NAT_WIN_H - 1), 0, 2 * NAT_WIN_H - 2)
    tab = rel_bias[:, row_off][:, :, :, col_off]
    ok = row_ok[:, None, :, :, None, None] & col_ok[None, None, None, None]
    tab = jnp.where(jnp.asarray(ok), tab.astype(F32)[None] * LOG2E, NEG_INF)
    tab = jnp.transpose(tab, (0, 1, 2, 4, 3, 5))
    return tab.reshape(3, H, NAT_Q_TOK, NAT_WIN_TOK)


def _neighborhood_attention(nq, rest, bias_tab):
    B, T, C = nq.shape
    n_blk = T // NAT_Q_TOK
    kind = lambda i: 1 - (i == 0).astype(jnp.int32) + (i == n_blk - 1).astype(jnp.int32)
    per = NAT_Q_TOK // NAT_EDGE_TOK
    n_edge = T // NAT_EDGE_TOK
    prev = lambda c: (lambda b, i: (b, jnp.maximum(i * per - 1, 0), c))
    cur = lambda c: (lambda b, i: (b, i, c))
    nxt = lambda c: (lambda b, i: (b, jnp.minimum((i + 1) * per, n_edge - 1), c))
    edge = lambda f: pl.BlockSpec((1, NAT_EDGE_TOK, C), f)
    mid = lambda f: pl.BlockSpec((1, NAT_Q_TOK, C), f)
    return pl.pallas_call(
        _nat_kernel,
        grid=(B, n_blk),
        in_specs=[mid(cur(0)),
                  edge(prev(0)), mid(cur(0)), edge(nxt(0)),
                  edge(prev(1)), mid(cur(1)), edge(nxt(1)),
                  pl.BlockSpec((1,) + bias_tab.shape[1:], lambda b, i: (kind(i), 0, 0, 0),
                               pipeline_mode=pl.Buffered(1))],
        out_specs=mid(cur(0)),
        out_shape=jax.ShapeDtypeStruct((B, T, C), BF16),
        scratch_shapes=[pltpu.VMEM((NAT_WIN_TOK, C), BF16), pltpu.VMEM((NAT_WIN_TOK, C), BF16)],
        compiler_params=_cparams(("arbitrary", "arbitrary")),
    )(nq, rest, rest, rest, rest, rest, rest, bias_tab)


DIFF_TQ = 1024
DIFF_TK = 512
DIFF_UNROLL = 8
DIFF_HEADROOM = 64.0


def _qk_head_mask():
    lane = lax.broadcasted_iota(jnp.int32, (1, LANES), 1)
    return (lane % (LANES // 2)) < HEAD_DIM // 2


def _diff_kernel(q_ref, k_ref, v_ref, lq1_ref, lk1_ref, lq2_ref, lk2_ref, g_ref, o_ref, a1_scr, a2_scr,
                 *, lam_init, n_kv):
    first = _qk_head_mask()
    q = q_ref[0]
    zero = jnp.zeros_like(q)
    q1 = jnp.where(first, q, zero)
    q2 = jnp.where(first, zero, q)
    tq = q.shape[0]
    ones = jnp.ones((DIFF_TK, LANES), BF16)
    zacc = jnp.zeros((tq, 2 * LANES), F32)

    def tile(c):
        k0 = pl.multiple_of(c * DIFF_TK, DIFF_TK)
        k = k_ref[0, pl.ds(k0, DIFF_TK), :]
        v1 = jnp.concatenate([v_ref[0, pl.ds(k0, DIFF_TK), :], ones], axis=1)
        return k, v1

    k, _ = tile(0)
    r1 = jnp.max(_dot_nt(q1, k), axis=-1, keepdims=True)
    r2 = jnp.max(_dot_nt(q2, k), axis=-1, keepdims=True)

    def fast(c, carry):
        a1, a2, t1, t2 = carry
        k, v1 = tile(c)

        def one(qm, r, a, t):
            s = _dot_nt(qm, k)
            for u in range(DIFF_TK // LANES):
                t = jnp.maximum(t, s[:, u * LANES:(u + 1) * LANES])
            return a + _dot(jnp.exp2(s - r).astype(BF16), v1), t

        a1, t1 = one(q1, r1, a1, t1)
        a2, t2 = one(q2, r2, a2, t2)
        return a1, a2, t1, t2

    tneg = jnp.full((tq, LANES), NEG_INF, F32)
    a1, a2, t1, t2 = lax.fori_loop(0, n_kv, fast, (zacc, zacc, tneg, tneg), unroll=DIFF_UNROLL)
    a1_scr[...] = a1
    a2_scr[...] = a2
    growth = jnp.max(jnp.maximum(t1 - r1, t2 - r2))

    @pl.when(jnp.logical_not(growth <= DIFF_HEADROOM))
    def _():
        def slow(c, carry):
            m1, b1, m2, b2 = carry
            k, v1 = tile(c)

            def one(qm, m, a):
                s = _dot_nt(qm, k)
                mn = jnp.maximum(m, jnp.max(s, axis=-1, keepdims=True))
                return mn, jnp.exp2(m - mn) * a + _dot(jnp.exp2(s - mn).astype(BF16), v1)

            m1, b1 = one(q1, m1, b1)
            m2, b2 = one(q2, m2, b2)
            return m1, b1, m2, b2

        neg = jnp.full((tq, 1), NEG_INF, F32)
        _, b1, _, b2 = lax.fori_loop(0, n_kv, slow, (neg, zacc, neg, zacc))
        a1_scr[...] = b1
        a2_scr[...] = b2

    lam = (jnp.exp(jnp.sum(lq1_ref[...] * lk1_ref[...], keepdims=True))
           - jnp.exp(jnp.sum(lq2_ref[...] * lk2_ref[...], keepdims=True)) + lam_init)
    o = (a1_scr[:, :LANES] / a1_scr[:, LANES:]) - lam * (a2_scr[:, :LANES] / a2_scr[:, LANES:])
    ms = jnp.mean(o * o, axis=-1, keepdims=True)
    o = o * lax.rsqrt(ms + EPS) * g_ref[...]
    o_ref[0] = (o * (1.0 - lam_init)).astype(BF16)


def _diff_attention(tok, rest, lq1, lk1, lq2, lk2, ln_g, lam_init):
    B, T, _ = tok.shape
    tq = min(DIFF_TQ, T)
    k_blk = 1024 // LANES
    v_blk = 1024 // LANES
    vec = lambda n: pl.BlockSpec((1, n), lambda b, h, i: (0, 0))
    return pl.pallas_call(
        functools.partial(_diff_kernel, lam_init=lam_init, n_kv=T // DIFF_TK),
        grid=(B, DIFF_HEADS, T // tq),
        in_specs=[pl.BlockSpec((1, tq, LANES), lambda b, h, i: (b, i, h)),
                  pl.BlockSpec((1, T, LANES), lambda b, h, i: (b, 0, k_blk + h)),
                  pl.BlockSpec((1, T, LANES), lambda b, h, i: (b, 0, v_blk + h)),
                  vec(HEAD_DIM), vec(HEAD_DIM), vec(HEAD_DIM), vec(HEAD_DIM), vec(LANES)],
        out_specs=pl.BlockSpec((1, tq, LANES), lambda b, h, i: (b, i, h)),
        out_shape=jax.ShapeDtypeStruct((B, T, DIFF_HEADS * LANES), BF16),
        scratch_shapes=[pltpu.VMEM((tq, 2 * LANES), F32), pltpu.VMEM((tq, 2 * LANES), F32)],
        compiler_params=_cparams(("arbitrary", "arbitrary", "arbitrary")),
    )(tok, tok, rest, lq1, lk1, lq2, lk2, ln_g)


DIL_TQ = 256
DIL_EDGE = 128


def _dil_kernel(q_ref, kp_ref, kc_ref, kn_ref, vp_ref, vc_ref, vn_ref, o_ref, lse_ref, kw, vw, *, seq, tq):
    i = pl.program_id(2)
    kw[0:DIL_EDGE, :] = kp_ref[0]
    kw[DIL_EDGE:DIL_EDGE + tq, :] = kc_ref[0]
    kw[DIL_EDGE + tq:, :] = kn_ref[0]
    vw[0:DIL_EDGE, :] = vp_ref[0]
    vw[DIL_EDGE:DIL_EDGE + tq, :] = vc_ref[0]
    vw[DIL_EDGE + tq:, :] = vn_ref[0]
    win = tq + 2 * DIL_EDGE
    s0 = i * tq
    qpos = s0 + lax.broadcasted_iota(jnp.int32, (tq, win), 0)
    kpos = s0 - DIL_EDGE + lax.broadcasted_iota(jnp.int32, (tq, win), 1)
    valid = (jnp.abs(kpos - qpos) <= DIL_SIDE) & (kpos >= 0) & (kpos < seq)
    first_qk = _qk_head_mask()
    first_v = _head_masks()
    ones = jnp.ones((win, LANES), BF16)
    for hp in range(DIL_HEADS // 2):
        cols = slice(hp * LANES, (hp + 1) * LANES)
        q = q_ref[0, :, cols]
        k = kw[:, cols]
        v1 = jnp.concatenate([vw[:, cols], ones], axis=1)
        outs, lses = [], []
        for hh in range(2):
            sel = first_qk if hh == 0 else jnp.logical_not(first_qk)
            qm = jnp.where(sel, q, jnp.zeros_like(q))
            s = jnp.where(valid, _dot_nt(qm, k), NEG_INF)
            m = jnp.max(s, axis=-1, keepdims=True)
            ov = _dot(jnp.exp2(s - m).astype(BF16), v1)
            l = ov[:, LANES:]
            outs.append(ov[:, :LANES] / l)
            lses.append(m + jnp.log2(l))
        o_ref[0, :, cols] = jnp.where(first_v, outs[0], outs[1]).astype(BF16)
        lse_ref[0, :, cols] = jnp.where(first_v, lses[0], lses[1])


def _dilated_group(qa, ka, va, qc, kc, vc, dil):
    B, seq, _ = qa.shape
    C = DIL_HEADS * HEAD_DIM
    tq = min(DIL_TQ, seq)
    per = tq // DIL_EDGE
    n_edge = seq // DIL_EDGE
    prev = lambda cf: (lambda b, r, i: (b, jnp.maximum(i * per - 1, 0), cf(r)))
    cur = lambda cf: (lambda b, r, i: (b, i, cf(r)))
    nxt = lambda cf: (lambda b, r, i: (b, jnp.minimum((i + 1) * per, n_edge - 1), cf(r)))
    edge = lambda f: pl.BlockSpec((1, DIL_EDGE, C), f)
    mid = lambda f: pl.BlockSpec((1, tq, C), f)
    out_map = lambda b, r, i: (b, i, r)
    return pl.pallas_call(
        functools.partial(_dil_kernel, seq=seq, tq=tq),
        grid=(B, dil, seq // tq),
        in_specs=[mid(cur(qc)),
                  edge(prev(kc)), mid(cur(kc)), edge(nxt(kc)),
                  edge(prev(vc)), mid(cur(vc)), edge(nxt(vc))],
        out_specs=[pl.BlockSpec((1, tq, C), out_map), pl.BlockSpec((1, tq, C), out_map)],
        out_shape=[jax.ShapeDtypeStruct((B, seq, dil * C), BF16),
                   jax.ShapeDtypeStruct((B, seq, dil * C), F32)],
        scratch_shapes=[pltpu.VMEM((tq + 2 * DIL_EDGE, C), BF16), pltpu.VMEM((tq + 2 * DIL_EDGE, C), BF16)],
        compiler_params=_cparams(("arbitrary", "arbitrary", "arbitrary")),
    )(qa, ka, ka, ka, va, va, va)


MERGE_CHUNK = 512


def _merge_kernel(x_ref, gt_ref, ya_ref, yb_ref, yc_ref, o0_ref, o1_ref, o2_ref, s0_ref, s1_ref, s2_ref,
                  g_ref, wb_ref, wo_ref, out_ref, o1_scr, s1_scr, o2_scr, s2_scr):
    tm = out_ref.shape[1]
    C = BRANCH_WIDTH
    D = out_ref.shape[2]
    n_j = D // MERGE_CHUNK
    for d, o_ref, s_ref, o_scr, s_scr in ((4, o1_ref, s1_ref, o1_scr, s1_scr),
                                          (16, o2_ref, s2_ref, o2_scr, s2_scr)):
        for r in range(d):
            for u in range(C // LANES):
                cols = slice(r * C + u * LANES, r * C + (u + 1) * LANES)
                o_scr[u, pl.ds(r, tm // d, stride=d), :] = o_ref[0, :, cols].astype(F32)
                s_scr[u, pl.ds(r, tm // d, stride=d), :] = s_ref[0, :, cols]
    wide = lambda scr: jnp.concatenate([scr[u] for u in range(C // LANES)], axis=1)
    s0 = s0_ref[0]
    s1 = wide(s1_scr)
    s2 = wide(s2_scr)
    m = jnp.maximum(jnp.maximum(s0, s1), s2)
    e0 = jnp.exp2(s0 - m)
    e1 = jnp.exp2(s1 - m)
    e2 = jnp.exp2(s2 - m)
    num = o0_ref[0].astype(F32) * e0 + wide(o1_scr) * e1 + wide(o2_scr) * e2
    ys = [ya_ref[0], yb_ref[0], yc_ref[0], (num / (e0 + e1 + e2)).astype(BF16)]

    acc = None
    for j in range(n_j):
        merged = None
        for n in range(N_BRANCH):
            lo = n * D + j * MERGE_CHUNK
            term = g_ref[0, :, lo:lo + MERGE_CHUNK].astype(F32) * _dot(ys[n], wb_ref[j, n])
            merged = term if merged is None else merged + term
        part = _dot(merged.astype(BF16), wo_ref[j])
        acc = part if acc is None else acc + part
    out_ref[0] = x_ref[0] + gt_ref[0] * acc


def _merge(x, gt, ya, yb, yc, dil_o, dil_lse, gates, wb, wo):
    B, T, D = x.shape
    tm = min(MERGE_TILE, T)
    C = BRANCH_WIDTH
    n_j = D // MERGE_CHUNK
    tok = pl.BlockSpec((1, tm, C), lambda b, i: (b, i, 0))
    res = lambda d: pl.BlockSpec((1, tm // d, d * C), lambda b, i: (b, i, 0))
    dil_specs = [res(d) for d in DIL_DILATIONS]
    return pl.pallas_call(
        _merge_kernel,
        grid=(B, T // tm),
        in_specs=[pl.BlockSpec((1, tm, D), lambda b, i: (b, i, 0)),
                  pl.BlockSpec((1, 1, D), lambda b, i: (b, 0, 0)),
                  tok, tok, tok, *dil_specs, *dil_specs,
                  pl.BlockSpec((1, tm, N_BRANCH * D), lambda b, i: (b, i, 0)),
                  pl.BlockSpec((n_j, N_BRANCH, C, MERGE_CHUNK), lambda b, i: (0, 0, 0, 0),
                               pipeline_mode=pl.Buffered(1)),
                  pl.BlockSpec((n_j, MERGE_CHUNK, D), lambda b, i: (0, 0, 0),
                               pipeline_mode=pl.Buffered(1))],
        out_specs=pl.BlockSpec((1, tm, D), lambda b, i: (b, i, 0)),
        out_shape=jax.ShapeDtypeStruct((B, T, D), F32),
        scratch_shapes=[pltpu.VMEM((C // LANES, tm, LANES), F32)] * 4,
        compiler_params=_cparams(("arbitrary", "arbitrary")),
    )(x, gt, ya, yb, yc, *dil_o, *dil_lse, gates, wb, wo)


def _prep_ffn(w_in, w_out):
    D = w_in.shape[0]
    pad = D_FF_PAD - D_FF
    a = jnp.pad(w_in[:, :D_FF].astype(BF16), ((0, 0), (0, pad)))
    b = jnp.pad(w_in[:, D_FF:].astype(BF16), ((0, 0), (0, pad)))
    n = D_FF_PAD // FF_CHUNK
    wab = jnp.concatenate([a.reshape(D, n, FF_CHUNK), b.reshape(D, n, FF_CHUNK)], axis=2)
    wo = jnp.pad(w_out.astype(BF16), ((0, pad), (0, 0)))
    return wab.reshape(D, n * 2 * FF_CHUNK), wo


def _qk_lane_order():
    rot_half = ROT_DIM // 2
    rest_half = (HEAD_DIM - ROT_DIM) // 2
    idx = []
    for n in range(LANES):
        half, w = divmod(n, LANES // 2)
        head, i = divmod(w, HEAD_DIM // 2)
        d = i + rot_half * half if i < rot_half else ROT_DIM + (i - rot_half) + rest_half * half
        idx.append(head * HEAD_DIM + d)
    return np.asarray(idx)


def _prep_w_in(w):
    W = BRANCH_WIDTH
    D = w.shape[0]
    qscale = HEAD_DIM ** -0.5 * LOG2E
    order = _qk_lane_order()
    qk = lambda cols: cols.reshape(D, -1, LANES)[:, :, order].reshape(D, -1)
    f_in, nq, nk, nv, dq, dk, dv = [w[:, n * W:(n + 1) * W] for n in range(7)]
    lq = [qk(w[:, (7 + g) * W:(8 + g) * W] * qscale) for g in range(3)]
    lk = [qk(w[:, (10 + g) * W:(11 + g) * W]) for g in range(3)]
    lv = [w[:, (13 + g) * W:(14 + g) * W] for g in range(3)]
    gates = w[:, MIX_COLS:]
    cols = [qk(dq * qscale), lq[0], lq[1], lq[2], qk(dk), lk[0], lk[1], lk[2],
            f_in, nq * qscale, nk, nv, dv, lv[0], lv[1], lv[2], gates]
    return jnp.concatenate(cols, axis=1).astype(BF16)


def _rope_tables(T):
    rot_half = ROT_DIM // 2
    inv = ROPE_THETA ** (-jnp.arange(0, ROT_DIM, 2, dtype=F32) / ROT_DIM)
    ang = jnp.arange(T, dtype=F32)[:, None] * inv[None, :]
    cos, sin = jnp.cos(ang), jnp.sin(ang)
    lane = np.arange(LANES)
    i = lane % (HEAD_DIM // 2)
    is_rot = jnp.asarray(i < rot_half)[None, :]
    src = np.minimum(i, rot_half - 1)
    sign = jnp.asarray(np.where(lane < LANES // 2, -1.0, 1.0), F32)[None, :]
    ra = jnp.where(is_rot, cos[:, src], 1.0)
    rb = jnp.where(is_rot, sin[:, src] * sign, 0.0)
    return ra, rb


def _trunk(x, mods, P, tabs):
    B, T, D = x.shape
    for l in range(DEPTH):
        sh1, sc1, gt1, sh2, sc2, gt2, sh3, sc3, gt3 = [mods[l][:, i:i + 1, :] for i in range(N_MOD)]
        x = _ffn(x, P["g_ffn1"][l], sh1, sc1, gt1, P["wab1"][l], P["wo1"][l], P["g_final"], False)
        tok, d4, d16, f_in, nq, rest, gates = _inproj(x, P["g_mix"][l], sh2, sc2, P["w_in"][l],
                                                      P["b_gate"][l], *tabs["rope"])
        ya = _fourier_mix(f_in, tabs["dft"])
        yb = _neighborhood_attention(nq, rest, P["nat_bias"][l])
        lam_init = 0.8 - 0.6 * math.exp(-0.3 * l)
        yc = _diff_attention(tok, rest, P["lam_q1"][l], P["lam_k1"][l], P["lam_q2"][l], P["lam_k2"][l],
                             P["diff_ln_g"][l], lam_init)
        dil = [_dilated_group(tok, tok, rest, lambda r: 1, lambda r: 3, lambda r: 3, 1),
               _dilated_group(d4, d4, d4, lambda r: r, lambda r: 4 + r, lambda r: 8 + r, 4),
               _dilated_group(d16, d16, d16, lambda r: r, lambda r: 16 + r, lambda r: 32 + r, 16)]
        x = _merge(x, gt2, ya, yb, yc, [o for o, _ in dil], [s for _, s in dil], gates,
                   P["w_branch"][l], P["w_out"][l])
        x = _ffn(x, P["g_ffn2"][l], sh3, sc3, gt3, P["wab2"][l], P["wo2"][l], P["g_final"], l == DEPTH - 1)
    return x


def kernel(x_prompt, x_sample, c_prompt, c_sample, w_ada, b_ada, g_ffn1, w_ffn1_in, w_ffn1_out, g_mix, w_in, b_gate, nat_rel_bias, lam_q1, lam_k1, lam_q2, lam_k2, diff_ln_g, w_branch, w_out, g_ffn2, w_ffn2_in, w_ffn2_out, g_final):
    L = DEPTH
    D = D_MODEL
    Bp, Bs = c_prompt.shape[0], c_sample.shape[0]
    rows = ((Bp + Bs + 7) // 8) * 8
    c_all = jnp.concatenate([c_prompt, c_sample, jnp.zeros((rows - Bp - Bs, D), F32)], axis=0)
    mods = _modulation(c_all, w_ada, b_ada).reshape(L, rows, N_MOD, D)

    P = {}
    ffn1 = [_prep_ffn(w_ffn1_in[l], w_ffn1_out[l]) for l in range(L)]
    ffn2 = [_prep_ffn(w_ffn2_in[l], w_ffn2_out[l]) for l in range(L)]
    P["wab1"] = [a for a, _ in ffn1]
    P["wo1"] = [b for _, b in ffn1]
    P["wab2"] = [a for a, _ in ffn2]
    P["wo2"] = [b for _, b in ffn2]
    P["w_in"] = [_prep_w_in(w_in[l]) for l in range(L)]
    P["b_gate"] = [b_gate[l].reshape(1, N_BRANCH * D) for l in range(L)]
    P["nat_bias"] = [_nat_bias_table(nat_rel_bias[l]) for l in range(L)]
    n_j = D // MERGE_CHUNK
    P["w_branch"] = [jnp.transpose(w_branch[l].astype(BF16).reshape(N_BRANCH, BRANCH_WIDTH, n_j, MERGE_CHUNK),
                                   (2, 0, 1, 3)) for l in range(L)]
    P["w_out"] = [w_out[l].astype(BF16).reshape(n_j, MERGE_CHUNK, D) for l in range(L)]
    for name, arr in (("g_ffn1", g_ffn1), ("g_mix", g_mix), ("g_ffn2", g_ffn2), ("lam_q1", lam_q1),
                      ("lam_k1", lam_k1), ("lam_q2", lam_q2), ("lam_k2", lam_k2), ("diff_ln_g", diff_ln_g)):
        P[name] = [arr[l].reshape(1, -1) for l in range(L)]
    P["g_final"] = g_final.reshape(1, D)

    outs = []
    for x, lo, hi in ((x_prompt, 0, Bp), (x_sample, Bp, Bp + Bs)):
        T = x.shape[1]
        tabs = {"rope": _rope_tables(T), "dft": _dft_tables(T)}
        outs.append(_trunk(x, mods[:, lo:hi], P, tabs))
    return tuple(outs)
```

```python
import functools
import math

import numpy as np
import jax
import jax.numpy as jnp
from jax import lax
from jax.experimental import pallas as pl
from jax.experimental.pallas import tpu as pltpu

F32 = jnp.float32
BF16 = jnp.bfloat16

D_MODEL = 2048
DEPTH = 2
GRID_W = 64
HEAD_DIM = 64
ROT_DIM = HEAD_DIM // 4
ROPE_THETA = 500000.0
EPS = 1e-6
HALF = 0.5
FNET_GROUP_DIM = 128
NAT_HEADS = 8
NAT_WIN_H = 8
NAT_WIN_W = 16
DIFF_HEADS = 4
DIL_DILATIONS = (1, 4, 16)
DIL_SIDE = 64
DIL_HEADS = 8
N_BRANCH = 4
BRANCH_WIDTH = 512
N_MOD = 9
D_FF = ((8 * D_MODEL // 3 + 63) // 64) * 64
MIX_COLS = 8192

LANES = 128
VMEM_LIMIT = 56 * 1024 * 1024
FF_CHUNK = 512
D_FF_PAD = ((D_FF + FF_CHUNK - 1) // FF_CHUNK) * FF_CHUNK
FFN_TILE = 512
TOKEN_TILE = 1024
MERGE_TILE = 256
PROJ_CHUNK = 1024
MXU_COLS = 256
NEG_INF = -1e30
LOG2E = math.log2(math.e)


def _cparams(sem):
    return pltpu.CompilerParams(dimension_semantics=sem, vmem_limit_bytes=VMEM_LIMIT)


def _dot(a, b):
    return jnp.dot(a, b, preferred_element_type=F32)


def _dot_nt(a, b):
    return lax.dot_general(a, b, (((1,), (1,)), ((), ())), preferred_element_type=F32)


def _norm_modulate(x, g, sh, sc):
    ms = jnp.mean(x * x, axis=-1, keepdims=True)
    y = x * lax.rsqrt(ms + EPS) * g
    return y * (1.0 + sc) + sh


def _mod_kernel(c_ref, w_ref, b_ref, o_ref):
    c = c_ref[...]
    s = c * jax.nn.sigmoid(c)
    o_ref[0] = _dot(s.astype(BF16), w_ref[0].astype(BF16)) + b_ref[0]


def _modulation(c_all, w_ada, b_ada):
    L, D, N = w_ada.shape
    R = c_all.shape[0]
    tn = 1152
    return pl.pallas_call(
        _mod_kernel,
        grid=(L, N // tn),
        in_specs=[
            pl.BlockSpec((R, D), lambda l, j: (0, 0)),
            pl.BlockSpec((1, D, tn), lambda l, j: (l, 0, j)),
            pl.BlockSpec((1, 1, tn), lambda l, j: (l, 0, j)),
        ],
        out_specs=pl.BlockSpec((1, R, tn), lambda l, j: (l, 0, j)),
        out_shape=jax.ShapeDtypeStruct((L, R, N), F32),
        compiler_params=_cparams(("arbitrary", "arbitrary")),
    )(c_all, w_ada, b_ada.reshape(L, 1, N))


def _ffn_kernel(x_ref, g_ref, sh_ref, sc_ref, gt_ref, wab0_ref, wab1_ref, wo0_ref, wo1_ref, gf_ref, o_ref,
                h_scr, acc_scr, *, final_norm, n_chunks):
    k = pl.program_id(2)
    last = pl.num_programs(2) - 1

    @pl.when(k == 0)
    def _():
        h = _norm_modulate(x_ref[0], g_ref[...], sh_ref[0], sc_ref[0])
        h_scr[...] = h.astype(BF16)
        acc_scr[...] = jnp.zeros_like(acc_scr)

    def chunk(wab_ref, wo_ref):
        ab = _dot(h_scr[...], wab_ref[...])
        a = ab[:, :FF_CHUNK]
        b = ab[:, FF_CHUNK:]
        act = (a * jax.nn.sigmoid(a)) * b
        return _dot(act.astype(BF16), wo_ref[...])

    def pair():
        acc_scr[...] += chunk(wab0_ref, wo0_ref) + chunk(wab1_ref, wo1_ref)

    def single():
        acc_scr[...] += chunk(wab0_ref, wo0_ref)

    if n_chunks % 2 == 0:
        pair()
    else:
        pl.when(k < last)(pair)
        pl.when(k == last)(single)

    @pl.when(k == last)
    def _():
        xn = x_ref[0] + (HALF * gt_ref[0]) * acc_scr[...]
        if final_norm:
            ms = jnp.mean(xn * xn, axis=-1, keepdims=True)
            xn = xn * lax.rsqrt(ms + EPS) * gf_ref[...]
        o_ref[0] = xn


def _ffn(x, g, sh, sc, gt, wab, wo, g_final, final_norm):
    B, T, D = x.shape
    tm = min(FFN_TILE, T)
    n_chunks = wo.shape[0] // FF_CHUNK
    nk = (n_chunks + 1) // 2
    vec = pl.BlockSpec((1, D), lambda b, i, k: (0, 0))
    per_b = pl.BlockSpec((1, 1, D), lambda b, i, k: (b, 0, 0))
    second = lambda k: jnp.minimum(2 * k + 1, n_chunks - 1)
    return pl.pallas_call(
        functools.partial(_ffn_kernel, final_norm=final_norm, n_chunks=n_chunks),
        grid=(B, T // tm, nk),
        in_specs=[
            pl.BlockSpec((1, tm, D), lambda b, i, k: (b, i, 0)),
            vec, per_b, per_b, per_b,
            pl.BlockSpec((D, 2 * FF_CHUNK), lambda b, i, k: (0, 2 * k)),
            pl.BlockSpec((D, 2 * FF_CHUNK), lambda b, i, k: (0, second(k))),
            pl.BlockSpec((FF_CHUNK, D), lambda b, i, k: (2 * k, 0)),
            pl.BlockSpec((FF_CHUNK, D), lambda b, i, k: (second(k), 0)),
            vec,
        ],
        out_specs=pl.BlockSpec((1, tm, D), lambda b, i, k: (b, i, 0)),
        out_shape=jax.ShapeDtypeStruct((B, T, D), F32),
        scratch_shapes=[pltpu.VMEM((tm, D), BF16), pltpu.VMEM((tm, D), F32)],
        compiler_params=_cparams(("arbitrary", "arbitrary", "arbitrary")),
    )(x, g, sh, sc, gt, wab, wab, wo, wo, g_final)


GATE_STEPS = N_BRANCH * D_MODEL // PROJ_CHUNK
FIRST_GATE_STEP = 8
DIL_SECTION = {d: d * BRANCH_WIDTH for d in DIL_DILATIONS}


def _rope_apply(z, ra, rb):
    return z * ra + pltpu.roll(z, LANES // 2, 1) * rb


def _inproj_kernel(x_ref, g_ref, sh_ref, sc_ref, w_ref, bg_ref, ra_ref, rb_ref,
                   tok_o, d4_o, d16_o, f_o, nq_o, rest_o, gate_o, h_scr, z_scr):
    j = pl.program_id(2)
    tm = h_scr.shape[0]
    n_sub = PROJ_CHUNK // MXU_COLS

    @pl.when(j == 0)
    def _():
        h = _norm_modulate(x_ref[0], g_ref[...], sh_ref[0], sc_ref[0])
        h_scr[...] = h.astype(BF16)

    def zsub(c):
        return _dot(h_scr[...], w_ref[:, c * MXU_COLS:(c + 1) * MXU_COLS])

    def roped(c):
        z = zsub(c)
        ra = ra_ref[...]
        rb = rb_ref[...]
        return [_rope_apply(z[:, u * LANES:(u + 1) * LANES], ra, rb) for u in range(MXU_COLS // LANES)]

    sub_groups = MXU_COLS // LANES
    dil_groups = BRANCH_WIDTH // LANES

    def scatter_residues():
        for d, o_ref, g0 in ((4, d4_o, 0), (16, d16_o, dil_groups)):
            for r in range(d):
                for u in range(dil_groups):
                    rows = z_scr[g0 + u, pl.ds(r, tm // d, stride=d), :]
                    lo = r * BRANCH_WIDTH + u * LANES
                    o_ref[0, :, lo:lo + LANES] = rows.astype(BF16)

    @pl.when((j == 0) | (j == 2))
    def _():
        for c in range(n_sub):
            for u, r in enumerate(roped(c)):
                lo = c * MXU_COLS + u * LANES
                tok_o[0, :, lo:lo + LANES] = r.astype(BF16)

    @pl.when((j == 1) | (j == 3))
    def _():
        for c in range(n_sub):
            for u, r in enumerate(roped(c)):
                z_scr[c * sub_groups + u] = r
        scatter_residues()

    @pl.when(j == 7)
    def _():
        for c in range(n_sub):
            z = zsub(c)
            for u in range(sub_groups):
                z_scr[c * sub_groups + u] = z[:, u * LANES:(u + 1) * LANES]
        scatter_residues()

    @pl.when(j == 4)
    def _():
        for c in range(n_sub):
            o_ref, lo = (f_o, c * MXU_COLS) if c < n_sub // 2 else (nq_o, (c - n_sub // 2) * MXU_COLS)
            o_ref[0, :, lo:lo + MXU_COLS] = zsub(c).astype(BF16)

    @pl.when((j == 5) | (j == 6))
    def _():
        for c in range(n_sub):
            rest_o[0, :, c * MXU_COLS:(c + 1) * MXU_COLS] = zsub(c).astype(BF16)

    @pl.when(j >= FIRST_GATE_STEP)
    def _():
        for c in range(n_sub):
            cols = slice(c * MXU_COLS, (c + 1) * MXU_COLS)
            pre = zsub(c) + bg_ref[:, cols]
            gate_o[0, :, cols] = (0.5 * jnp.tanh(0.5 * pre) + 0.5).astype(BF16)


def _inproj(x, g, sh, sc, w, bg, ra, rb):
    B, T, D = x.shape
    tm = min(TOKEN_TILE, T)
    n_steps = w.shape[1] // PROJ_CHUNK
    vec = pl.BlockSpec((1, D), lambda b, i, j: (0, 0))
    per_b = pl.BlockSpec((1, 1, D), lambda b, i, j: (b, 0, 0))
    tab = pl.BlockSpec((tm, LANES), lambda b, i, j: (i, 0))
    step = lambda j, *edges: sum((j >= e).astype(jnp.int32) for e in edges)
    dil_spec = lambda d: pl.BlockSpec((1, tm // d, DIL_SECTION[d]), lambda b, i, j: (b, i, step(j, 3, 7)))
    tok512 = pl.BlockSpec((1, tm, BRANCH_WIDTH), lambda b, i, j: (b, i, 0))
    return pl.pallas_call(
        _inproj_kernel,
        grid=(B, T // tm, n_steps),
        in_specs=[
            pl.BlockSpec((1, tm, D), lambda b, i, j: (b, i, 0), pipeline_mode=pl.Buffered(1)),
            vec, per_b, per_b,
            pl.BlockSpec((D, PROJ_CHUNK), lambda b, i, j: (0, j)),
            pl.BlockSpec((1, PROJ_CHUNK), lambda b, i, j: (0, jnp.maximum(j - FIRST_GATE_STEP, 0))),
            tab, tab,
        ],
        out_specs=[
            pl.BlockSpec((1, tm, PROJ_CHUNK), lambda b, i, j: (b, i, step(j, 2))),
            dil_spec(4), dil_spec(16),
            tok512, tok512,
            pl.BlockSpec((1, tm, PROJ_CHUNK), lambda b, i, j: (b, i, step(j, 6))),
            pl.BlockSpec((1, tm, PROJ_CHUNK), lambda b, i, j: (b, i, jnp.maximum(j - FIRST_GATE_STEP, 0))),
        ],
        out_shape=[
            jax.ShapeDtypeStruct((B, T, 2 * PROJ_CHUNK), BF16),
            jax.ShapeDtypeStruct((B, T // 4, 3 * DIL_SECTION[4]), BF16),
            jax.ShapeDtypeStruct((B, T // 16, 3 * DIL_SECTION[16]), BF16),
            jax.ShapeDtypeStruct((B, T, BRANCH_WIDTH), BF16),
            jax.ShapeDtypeStruct((B, T, BRANCH_WIDTH), BF16),
            jax.ShapeDtypeStruct((B, T, 2 * PROJ_CHUNK), BF16),
            jax.ShapeDtypeStruct((B, T, GATE_STEPS * PROJ_CHUNK), BF16),
        ],
        scratch_shapes=[pltpu.VMEM((tm, D), BF16), pltpu.VMEM((PROJ_CHUNK // LANES, tm, LANES), F32)],
        compiler_params=_cparams(("arbitrary", "arbitrary", "arbitrary")),
    )(x, g, sh, sc, w, bg, ra, rb)


FFT_T2 = 128
FFT_COLS = 2048
FFT_K1_BLOCK = 4


def _fft1_kernel(u_ref, c1_ref, s1_ref, ar_ref, ai_ref):
    u = u_ref[0]
    ar_ref[0] = _dot(c1_ref[...], u).astype(BF16)
    ai_ref[0] = (-_dot(s1_ref[...], u)).astype(BF16)


def _fft2_kernel(ar_ref, ai_ref, twc_ref, tws_ref, c2_ref, s2_ref, cc_ref, sc_ref, o_ref, *, norm):
    c2 = c2_ref[...]
    s2 = s2_ref[...]
    cc = cc_ref[...]
    sc = sc_ref[...]
    for kk in range(FFT_K1_BLOCK):
        rows = slice(kk * FFT_T2, (kk + 1) * FFT_T2)
        ar = ar_ref[0, rows, :].astype(F32)
        ai = ai_ref[0, rows, :].astype(F32)
        twc = jnp.concatenate([twc_ref[rows, :]] * (BRANCH_WIDTH // LANES), axis=1)
        tws = jnp.concatenate([tws_ref[rows, :]] * (BRANCH_WIDTH // LANES), axis=1)
        br = (ar * twc + ai * tws).astype(BF16)
        bi = (ai * twc - ar * tws).astype(BF16)
        zr = _dot(c2, br) + _dot(s2, bi)
        zi = _dot(c2, bi) - _dot(s2, br)
        y = _dot(zr.astype(BF16), cc) + _dot(zi.astype(BF16), sc)
        o_ref[0, :, kk * BRANCH_WIDTH:(kk + 1) * BRANCH_WIDTH] = (y * norm).astype(BF16)


def _dft_tables(T):
    T1 = T // FFT_T2
    k1 = np.arange(T1)
    a1 = 2.0 * np.pi * np.outer(k1, k1) / T1
    k2 = np.arange(FFT_T2)
    a2 = 2.0 * np.pi * np.outer(k2, k2) / FFT_T2
    atw = 2.0 * np.pi * np.outer(k1, k2).reshape(T, 1) / T
    atw = np.broadcast_to(atw, (T, LANES))
    ch = np.arange(FNET_GROUP_DIM)
    ac = 2.0 * np.pi * np.outer(ch, ch) / FNET_GROUP_DIM
    eye = np.eye(BRANCH_WIDTH // FNET_GROUP_DIM)
    bf = lambda a: jnp.asarray(a, dtype=BF16)
    return dict(c1=bf(np.cos(a1)), s1=bf(np.sin(a1)), c2=bf(np.cos(a2)), s2=bf(np.sin(a2)),
                twc=jnp.asarray(np.cos(atw), F32), tws=jnp.asarray(np.sin(atw), F32),
                cc=bf(np.kron(eye, np.cos(ac))), sc=bf(np.kron(eye, np.sin(ac))))


def _fourier_mix(f, tabs):
    B, T, C = f.shape
    T1 = T // FFT_T2
    n_col = FFT_T2 * C // FFT_COLS
    full2 = lambda shape: pl.BlockSpec(shape, lambda b, i: (0, 0))
    ar, ai = pl.pallas_call(
        _fft1_kernel,
        grid=(B, n_col),
        in_specs=[pl.BlockSpec((1, T1, FFT_COLS), lambda b, i: (b, 0, i)),
                  full2((T1, T1)), full2((T1, T1))],
        out_specs=[pl.BlockSpec((1, T1, FFT_COLS), lambda b, i: (b, 0, i))] * 2,
        out_shape=[jax.ShapeDtypeStruct((B, T1, FFT_T2 * C), BF16)] * 2,
        compiler_params=_cparams(("arbitrary", "arbitrary")),
    )(f.reshape(B, T1, FFT_T2 * C), tabs["c1"], tabs["s1"])
    ar = ar.reshape(B, T, C)
    ai = ai.reshape(B, T, C)
    rows = FFT_K1_BLOCK * FFT_T2
    y = pl.pallas_call(
        functools.partial(_fft2_kernel, norm=1.0 / math.sqrt(T * FNET_GROUP_DIM)),
        grid=(B, T1 // FFT_K1_BLOCK),
        in_specs=[pl.BlockSpec((1, rows, C), lambda b, i: (b, i, 0)),
                  pl.BlockSpec((1, rows, C), lambda b, i: (b, i, 0)),
                  pl.BlockSpec((rows, LANES), lambda b, i: (i, 0)),
                  pl.BlockSpec((rows, LANES), lambda b, i: (i, 0)),
                  full2((FFT_T2, FFT_T2)), full2((FFT_T2, FFT_T2)),
                  full2((C, C)), full2((C, C))],
        out_specs=pl.BlockSpec((1, FFT_T2, FFT_K1_BLOCK * C), lambda b, i: (b, 0, i)),
        out_shape=jax.ShapeDtypeStruct((B, FFT_T2, T1 * C), BF16),
        compiler_params=_cparams(("arbitrary", "arbitrary")),
    )(ar, ai, tabs["twc"], tabs["tws"], tabs["c2"], tabs["s2"], tabs["cc"], tabs["sc"])
    return y.reshape(B, T, C)


NAT_Q_ROWS = 8
NAT_Q_TOK = NAT_Q_ROWS * GRID_W
NAT_EDGE_TOK = (NAT_WIN_H // 2) * GRID_W
NAT_WIN_TOK = NAT_Q_TOK + 2 * NAT_EDGE_TOK


def _head_masks():
    lane = lax.broadcasted_iota(jnp.int32, (1, LANES), 1)
    return lane < HEAD_DIM


def _nat_kernel(q_ref, kp_ref, kc_ref, kn_ref, vp_ref, vc_ref, vn_ref, bias_ref, o_ref, kw, vw):
    kw[0:NAT_EDGE_TOK, :] = kp_ref[0]
    kw[NAT_EDGE_TOK:NAT_EDGE_TOK + NAT_Q_TOK, :] = kc_ref[0]
    kw[NAT_EDGE_TOK + NAT_Q_TOK:, :] = kn_ref[0]
    vw[0:NAT_EDGE_TOK, :] = vp_ref[0]
    vw[NAT_EDGE_TOK:NAT_EDGE_TOK + NAT_Q_TOK, :] = vc_ref[0]
    vw[NAT_EDGE_TOK + NAT_Q_TOK:, :] = vn_ref[0]
    first = _head_masks()
    ones = jnp.ones((NAT_WIN_TOK, LANES), BF16)
    for hp in range(NAT_HEADS // 2):
        cols = slice(hp * LANES, (hp + 1) * LANES)
        q = q_ref[0, :, cols]
        k = kw[:, cols]
        v1 = jnp.concatenate([vw[:, cols], ones], axis=1)
        outs = []
        for hh in range(2):
            sel = first if hh == 0 else jnp.logical_not(first)
            qm = jnp.where(sel, q, jnp.zeros_like(q))
            s = _dot_nt(qm, k) + bias_ref[0, 2 * hp + hh]
            m = jnp.max(s, axis=-1, keepdims=True)
            ov = _dot(jnp.exp2(s - m).astype(BF16), v1)
            outs.append(ov[:, :LANES] / ov[:, LANES:])
        o_ref[0, :, cols] = jnp.where(first, outs[0], outs[1]).astype(BF16)


def _nat_bias_table(rel_bias):
    H = rel_bias.shape[0]
    half = NAT_WIN_H // 2
    col = np.arange(GRID_W)
    col_start = np.clip(col - NAT_WIN_W // 2, 0, GRID_W - NAT_WIN_W)
    kc = np.arange(GRID_W)
    col_ok = (kc[None, :] >= col_start[:, None]) & (kc[None, :] < col_start[:, None] + NAT_WIN_W)
    col_off = np.clip(kc[None, :] - col[:, None] + (NAT_WIN_W - 1), 0, 2 * NAT_WIN_W - 2)
    rr = np.arange(NAT_Q_ROWS)
    wr = np.arange(NAT_Q_ROWS + NAT_WIN_H)
    start = np.stack([np.maximum(rr - half, 0) + half, rr, np.minimum(rr, half)])
    row_ok = (wr[None, None, :] >= start[:, :, None]) & (wr[None, None, :] < start[:, :, None] + NAT_WIN_H)
    row_off = np.clip(wr[None, :] - half - rr[:, None] + (NAT_WIN_H - 1), 0, 2 * NAT_WIN_H - 2)
    tab = rel_bias[:, row_off][:, :, :, col_off]
    ok = row_ok[:, None, :, :, None, None] & col_ok[None, None, None, None]
    tab = jnp.where(jnp.asarray(ok), tab.astype(F32)[None] * LOG2E, NEG_INF)
    tab = jnp.transpose(tab, (0, 1, 2, 4, 3, 5))
    return tab.reshape(3, H, NAT_Q_TOK, NAT_WIN_TOK)


def _neighborhood_attention(nq, rest, bias_tab):
    B, T, C = nq.shape
    n_blk = T // NAT_Q_TOK
    kind = lambda i: 1 - (i == 0).astype(jnp.int32) + (i == n_blk - 1).astype(jnp.int32)
    per = NAT_Q_TOK // NAT_EDGE_TOK
    n_edge = T // NAT_EDGE_TOK
    prev = lambda c: (lambda b, i: (b, jnp.maximum(i * per - 1, 0), c))
    cur = lambda c: (lambda b, i: (b, i, c))
    nxt = lambda c: (lambda b, i: (b, jnp.minimum((i + 1) * per, n_edge - 1), c))
    edge = lambda f: pl.BlockSpec((1, NAT_EDGE_TOK, C), f)
    mid = lambda f: pl.BlockSpec((1, NAT_Q_TOK, C), f)
    return pl.pallas_call(
        _nat_kernel,
        grid=(B, n_blk),
        in_specs=[mid(cur(0)),
                  edge(prev(0)), mid(cur(0)), edge(nxt(0)),
                  edge(prev(1)), mid(cur(1)), edge(nxt(1)),
                  pl.BlockSpec((1,) + bias_tab.shape[1:], lambda b, i: (kind(i), 0, 0, 0),
                               pipeline_mode=pl.Buffered(1))],
        out_specs=mid(cur(0)),
        out_shape=jax.ShapeDtypeStruct((B, T, C), BF16),
        scratch_shapes=[pltpu.VMEM((NAT_WIN_TOK, C), BF16), pltpu.VMEM((NAT_WIN_TOK, C), BF16)],
        compiler_params=_cparams(("arbitrary", "arbitrary")),
    )(nq, rest, rest, rest, rest, rest, rest, bias_tab)


DIFF_TQ = 1024
DIFF_TK = 512
DIFF_UNROLL = 16
DIFF_HEADROOM = 64.0


def _qk_head_mask():
    lane = lax.broadcasted_iota(jnp.int32, (1, LANES), 1)
    return (lane % (LANES // 2)) < HEAD_DIM // 2


def _diff_kernel(q_ref, k_ref, v_ref, lq1_ref, lk1_ref, lq2_ref, lk2_ref, g_ref, o_ref, a1_scr, a2_scr,
                 *, lam_init, n_kv):
    first = _qk_head_mask()
    q = q_ref[0]
    zero = jnp.zeros_like(q)
    q1 = jnp.where(first, q, zero)
    q2 = jnp.where(first, zero, q)
    tq = q.shape[0]
    ones = jnp.ones((DIFF_TK, LANES), BF16)
    zacc = jnp.zeros((tq, 2 * LANES), F32)

    def tile(c):
        k0 = pl.multiple_of(c * DIFF_TK, DIFF_TK)
        k = k_ref[0, pl.ds(k0, DIFF_TK), :]
        v1 = jnp.concatenate([v_ref[0, pl.ds(k0, DIFF_TK), :], ones], axis=1)
        return k, v1

    k, _ = tile(0)
    r1 = jnp.max(_dot_nt(q1, k), axis=-1, keepdims=True)
    r2 = jnp.max(_dot_nt(q2, k), axis=-1, keepdims=True)

    def fast(c, carry):
        a1, a2, t1, t2 = carry
        k, v1 = tile(c)

        def one(qm, r, a, t):
            s = _dot_nt(qm, k)
            for u in range(DIFF_TK // LANES):
                t = jnp.maximum(t, s[:, u * LANES:(u + 1) * LANES])
            return a + _dot(jnp.exp2(s - r).astype(BF16), v1), t

        a1, t1 = one(q1, r1, a1, t1)
        a2, t2 = one(q2, r2, a2, t2)
        return a1, a2, t1, t2

    tneg = jnp.full((tq, LANES), NEG_INF, F32)
    a1, a2, t1, t2 = lax.fori_loop(0, n_kv, fast, (zacc, zacc, tneg, tneg), unroll=DIFF_UNROLL)
    a1_scr[...] = a1
    a2_scr[...] = a2
    growth = jnp.max(jnp.maximum(t1 - r1, t2 - r2))

    @pl.when(jnp.logical_not(growth <= DIFF_HEADROOM))
    def _():
        def slow(c, carry):
            m1, b1, m2, b2 = carry
            k, v1 = tile(c)

            def one(qm, m, a):
                s = _dot_nt(qm, k)
                mn = jnp.maximum(m, jnp.max(s, axis=-1, keepdims=True))
                return mn, jnp.exp2(m - mn) * a + _dot(jnp.exp2(s - mn).astype(BF16), v1)

            m1, b1 = one(q1, m1, b1)
            m2, b2 = one(q2, m2, b2)
            return m1, b1, m2, b2

        neg = jnp.full((tq, 1), NEG_INF, F32)
        _, b1, _, b2 = lax.fori_loop(0, n_kv, slow, (neg, zacc, neg, zacc))
        a1_scr[...] = b1
        a2_scr[...] = b2

    lam = (jnp.exp(jnp.sum(lq1_ref[...] * lk1_ref[...], keepdims=True))
           - jnp.exp(jnp.sum(lq2_ref[...] * lk2_ref[...], keepdims=True)) + lam_init)
    o = (a1_scr[:, :LANES] / a1_scr[:, LANES:]) - lam * (a2_scr[:, :LANES] / a2_scr[:, LANES:])
    ms = jnp.mean(o * o, axis=-1, keepdims=True)
    o = o * lax.rsqrt(ms + EPS) * g_ref[...]
    o_ref[0] = (o * (1.0 - lam_init)).astype(BF16)


def _diff_attention(tok, rest, lq1, lk1, lq2, lk2, ln_g, lam_init):
    B, T, _ = tok.shape
    tq = min(DIFF_TQ, T)
    k_blk = 1024 // LANES
    v_blk = 1024 // LANES
    vec = lambda n: pl.BlockSpec((1, n), lambda b, h, i: (0, 0))
    return pl.pallas_call(
        functools.partial(_diff_kernel, lam_init=lam_init, n_kv=T // DIFF_TK),
        grid=(B, DIFF_HEADS, T // tq),
        in_specs=[pl.BlockSpec((1, tq, LANES), lambda b, h, i: (b, i, h)),
                  pl.BlockSpec((1, T, LANES), lambda b, h, i: (b, 0, k_blk + h)),
                  pl.BlockSpec((1, T, LANES), lambda b, h, i: (b, 0, v_blk + h)),
                  vec(HEAD_DIM), vec(HEAD_DIM), vec(HEAD_DIM), vec(HEAD_DIM), vec(LANES)],
        out_specs=pl.BlockSpec((1, tq, LANES), lambda b, h, i: (b, i, h)),
        out_shape=jax.ShapeDtypeStruct((B, T, DIFF_HEADS * LANES), BF16),
        scratch_shapes=[pltpu.VMEM((tq, 2 * LANES), F32), pltpu.VMEM((tq, 2 * LANES), F32)],
        compiler_params=_cparams(("arbitrary", "arbitrary", "arbitrary")),
    )(tok, tok, rest, lq1, lk1, lq2, lk2, ln_g)


DIL_TQ = 256
DIL_EDGE = 128


def _dil_kernel(q_ref, kp_ref, kc_ref, kn_ref, vp_ref, vc_ref, vn_ref, o_ref, lse_ref, kw, vw, *, seq, tq):
    i = pl.program_id(2)
    kw[0:DIL_EDGE, :] = kp_ref[0]
    kw[DIL_EDGE:DIL_EDGE + tq, :] = kc_ref[0]
    kw[DIL_EDGE + tq:, :] = kn_ref[0]
    vw[0:DIL_EDGE, :] = vp_ref[0]
    vw[DIL_EDGE:DIL_EDGE + tq, :] = vc_ref[0]
    vw[DIL_EDGE + tq:, :] = vn_ref[0]
    win = tq + 2 * DIL_EDGE
    s0 = i * tq
    qpos = s0 + lax.broadcasted_iota(jnp.int32, (tq, win), 0)
    kpos = s0 - DIL_EDGE + lax.broadcasted_iota(jnp.int32, (tq, win), 1)
    valid = (jnp.abs(kpos - qpos) <= DIL_SIDE) & (kpos >= 0) & (kpos < seq)
    first_qk = _qk_head_mask()
    first_v = _head_masks()
    ones = jnp.ones((win, LANES), BF16)
    for hp in range(DIL_HEADS // 2):
        cols = slice(hp * LANES, (hp + 1) * LANES)
        q = q_ref[0, :, cols]
        k = kw[:, cols]
        v1 = jnp.concatenate([vw[:, cols], ones], axis=1)
        outs, lses = [], []
        for hh in range(2):
            sel = first_qk if hh == 0 else jnp.logical_not(first_qk)
            qm = jnp.where(sel, q, jnp.zeros_like(q))
            s = jnp.where(valid, _dot_nt(qm, k), NEG_INF)
            m = jnp.max(s, axis=-1, keepdims=True)
            ov = _dot(jnp.exp2(s - m).astype(BF16), v1)
            l = ov[:, LANES:]
            outs.append(ov[:, :LANES] / l)
            lses.append(m + jnp.log2(l))
        o_ref[0, :, cols] = jnp.where(first_v, outs[0], outs[1]).astype(BF16)
        lse_ref[0, :, cols] = jnp.where(first_v, lses[0], lses[1])


def _dilated_group(qa, ka, va, qc, kc, vc, dil):
    B, seq, _ = qa.shape
    C = DIL_HEADS * HEAD_DIM
    tq = min(DIL_TQ, seq)
    per = tq // DIL_EDGE
    n_edge = seq // DIL_EDGE
    prev = lambda cf: (lambda b, r, i: (b, jnp.maximum(i * per - 1, 0), cf(r)))
    cur = lambda cf: (lambda b, r, i: (b, i, cf(r)))
    nxt = lambda cf: (lambda b, r, i: (b, jnp.minimum((i + 1) * per, n_edge - 1), cf(r)))
    edge = lambda f: pl.BlockSpec((1, DIL_EDGE, C), f)
    mid = lambda f: pl.BlockSpec((1, tq, C), f)
    out_map = lambda b, r, i: (b, i, r)
    return pl.pallas_call(
        functools.partial(_dil_kernel, seq=seq, tq=tq),
        grid=(B, dil, seq // tq),
        in_specs=[mid(cur(qc)),
                  edge(prev(kc)), mid(cur(kc)), edge(nxt(kc)),
                  edge(prev(vc)), mid(cur(vc)), edge(nxt(vc))],
        out_specs=[pl.BlockSpec((1, tq, C), out_map), pl.BlockSpec((1, tq, C), out_map)],
        out_shape=[jax.ShapeDtypeStruct((B, seq, dil * C), BF16),
                   jax.ShapeDtypeStruct((B, seq, dil * C), F32)],
        scratch_shapes=[pltpu.VMEM((tq + 2 * DIL_EDGE, C), BF16), pltpu.VMEM((tq + 2 * DIL_EDGE, C), BF16)],
        compiler_params=_cparams(("arbitrary", "arbitrary", "arbitrary")),
    )(qa, ka, ka, ka, va, va, va)


MERGE_CHUNK = 512


def _merge_kernel(x_ref, gt_ref, ya_ref, yb_ref, yc_ref, o0_ref, o1_ref, o2_ref, s0_ref, s1_ref, s2_ref,
                  g_ref, wb_ref, wo_ref, out_ref, o1_scr, s1_scr, o2_scr, s2_scr):
    tm = out_ref.shape[1]
    C = BRANCH_WIDTH
    D = out_ref.shape[2]
    n_j = D // MERGE_CHUNK
    for d, o_ref, s_ref, o_scr, s_scr in ((4, o1_ref, s1_ref, o1_scr, s1_scr),
                                          (16, o2_ref, s2_ref, o2_scr, s2_scr)):
        for r in range(d):
            for u in range(C // LANES):
                cols = slice(r * C + u * LANES, r * C + (u + 1) * LANES)
                o_scr[u, pl.ds(r, tm // d, stride=d), :] = o_ref[0, :, cols].astype(F32)
                s_scr[u, pl.ds(r, tm // d, stride=d), :] = s_ref[0, :, cols]
    wide = lambda scr: jnp.concatenate([scr[u] for u in range(C // LANES)], axis=1)
    s0 = s0_ref[0]
    s1 = wide(s1_scr)
    s2 = wide(s2_scr)
    m = jnp.maximum(jnp.maximum(s0, s1), s2)
    e0 = jnp.exp2(s0 - m)
    e1 = jnp.exp2(s1 - m)
    e2 = jnp.exp2(s2 - m)
    num = o0_ref[0].astype(F32) * e0 + wide(o1_scr) * e1 + wide(o2_scr) * e2
    ys = [ya_ref[0], yb_ref[0], yc_ref[0], (num / (e0 + e1 + e2)).astype(BF16)]

    acc = None
    for j in range(n_j):
        merged = None
        for n in range(N_BRANCH):
            lo = n * D + j * MERGE_CHUNK
            term = g_ref[0, :, lo:lo + MERGE_CHUNK].astype(F32) * _dot(ys[n], wb_ref[j, n])
            merged = term if merged is None else merged + term
        part = _dot(merged.astype(BF16), wo_ref[j])
        acc = part if acc is None else acc + part
    out_ref[0] = x_ref[0] + gt_ref[0] * acc


def _merge(x, gt, ya, yb, yc, dil_o, dil_lse, gates, wb, wo):
    B, T, D = x.shape
    tm = min(MERGE_TILE, T)
    C = BRANCH_WIDTH
    n_j = D // MERGE_CHUNK
    tok = pl.BlockSpec((1, tm, C), lambda b, i: (b, i, 0))
    res = lambda d: pl.BlockSpec((1, tm // d, d * C), lambda b, i: (b, i, 0))
    dil_specs = [res(d) for d in DIL_DILATIONS]
    return pl.pallas_call(
        _merge_kernel,
        grid=(B, T // tm),
        in_specs=[pl.BlockSpec((1, tm, D), lambda b, i: (b, i, 0)),
                  pl.BlockSpec((1, 1, D), lambda b, i: (b, 0, 0)),
                  tok, tok, tok, *dil_specs, *dil_specs,
                  pl.BlockSpec((1, tm, N_BRANCH * D), lambda b, i: (b, i, 0)),
                  pl.BlockSpec((n_j, N_BRANCH, C, MERGE_CHUNK), lambda b, i: (0, 0, 0, 0),
                               pipeline_mode=pl.Buffered(1)),
                  pl.BlockSpec((n_j, MERGE_CHUNK, D), lambda b, i: (0, 0, 0),
                               pipeline_mode=pl.Buffered(1))],
        out_specs=pl.BlockSpec((1, tm, D), lambda b, i: (b, i, 0)),
        out_shape=jax.ShapeDtypeStruct((B, T, D), F32),
        scratch_shapes=[pltpu.VMEM((C // LANES, tm, LANES), F32)] * 4,
        compiler_params=_cparams(("arbitrary", "arbitrary")),
    )(x, gt, ya, yb, yc, *dil_o, *dil_lse, gates, wb, wo)


def _prep_ffn(w_in, w_out):
    D = w_in.shape[0]
    pad = D_FF_PAD - D_FF
    a = jnp.pad(w_in[:, :D_FF].astype(BF16), ((0, 0), (0, pad)))
    b = jnp.pad(w_in[:, D_FF:].astype(BF16), ((0, 0), (0, pad)))
    n = D_FF_PAD // FF_CHUNK
    wab = jnp.concatenate([a.reshape(D, n, FF_CHUNK), b.reshape(D, n, FF_CHUNK)], axis=2)
    wo = jnp.pad(w_out.astype(BF16), ((0, pad), (0, 0)))
    return wab.reshape(D, n * 2 * FF_CHUNK), wo


def _qk_lane_order():
    rot_half = ROT_DIM // 2
    rest_half = (HEAD_DIM - ROT_DIM) // 2
    idx = []
    for n in range(LANES):
        half, w = divmod(n, LANES // 2)
        head, i = divmod(w, HEAD_DIM // 2)
        d = i + rot_half * half if i < rot_half else ROT_DIM + (i - rot_half) + rest_half * half
        idx.append(head * HEAD_DIM + d)
    return np.asarray(idx)


def _prep_w_in(w):
    W = BRANCH_WIDTH
    D = w.shape[0]
    qscale = HEAD_DIM ** -0.5 * LOG2E
    order = _qk_lane_order()
    qk = lambda cols: cols.reshape(D, -1, LANES)[:, :, order].reshape(D, -1)
    f_in, nq, nk, nv, dq, dk, dv = [w[:, n * W:(n + 1) * W] for n in range(7)]
    lq = [qk(w[:, (7 + g) * W:(8 + g) * W] * qscale) for g in range(3)]
    lk = [qk(w[:, (10 + g) * W:(11 + g) * W]) for g in range(3)]
    lv = [w[:, (13 + g) * W:(14 + g) * W] for g in range(3)]
    gates = w[:, MIX_COLS:]
    cols = [qk(dq * qscale), lq[0], lq[1], lq[2], qk(dk), lk[0], lk[1], lk[2],
            f_in, nq * qscale, nk, nv, dv, lv[0], lv[1], lv[2], gates]
    return jnp.concatenate(cols, axis=1).astype(BF16)


def _rope_tables(T):
    rot_half = ROT_DIM // 2
    inv = ROPE_THETA ** (-jnp.arange(0, ROT_DIM, 2, dtype=F32) / ROT_DIM)
    ang = jnp.arange(T, dtype=F32)[:, None] * inv[None, :]
    cos, sin = jnp.cos(ang), jnp.sin(ang)
    lane = np.arange(LANES)
    i = lane % (HEAD_DIM // 2)
    is_rot = jnp.asarray(i < rot_half)[None, :]
    src = np.minimum(i, rot_half - 1)
    sign = jnp.asarray(np.where(lane < LANES // 2, -1.0, 1.0), F32)[None, :]
    ra = jnp.where(is_rot, cos[:, src], 1.0)
    rb = jnp.where(is_rot, sin[:, src] * sign, 0.0)
    return ra, rb


def _trunk(x, mods, P, tabs):
    B, T, D = x.shape
    for l in range(DEPTH):
        sh1, sc1, gt1, sh2, sc2, gt2, sh3, sc3, gt3 = [mods[l][:, i:i + 1, :] for i in range(N_MOD)]
        x = _ffn(x, P["g_ffn1"][l], sh1, sc1, gt1, P["wab1"][l], P["wo1"][l], P["g_final"], False)
        tok, d4, d16, f_in, nq, rest, gates = _inproj(x, P["g_mix"][l], sh2, sc2, P["w_in"][l],
                                                      P["b_gate"][l], *tabs["rope"])
        ya = _fourier_mix(f_in, tabs["dft"])
        yb = _neighborhood_attention(nq, rest, P["nat_bias"][l])
        lam_init = 0.8 - 0.6 * math.exp(-0.3 * l)
        yc = _diff_attention(tok, rest, P["lam_q1"][l], P["lam_k1"][l], P["lam_q2"][l], P["lam_k2"][l],
                             P["diff_ln_g"][l], lam_init)
        dil = [_dilated_group(tok, tok, rest, lambda r: 1, lambda r: 3, lambda r: 3, 1),
               _dilated_group(d4, d4, d4, lambda r: r, lambda r: 4 + r, lambda r: 8 + r, 4),
               _dilated_group(d16, d16, d16, lambda r: r, lambda r: 16 + r, lambda r: 32 + r, 16)]
        x = _merge(x, gt2, ya, yb, yc, [o for o, _ in dil], [s for _, s in dil], gates,
                   P["w_branch"][l], P["w_out"][l])
        x = _ffn(x, P["g_ffn2"][l], sh3, sc3, gt3, P["wab2"][l], P["wo2"][l], P["g_final"], l == DEPTH - 1)
    return x


def kernel(x_prompt, x_sample, c_prompt, c_sample, w_ada, b_ada, g_ffn1, w_ffn1_in, w_ffn1_out, g_mix, w_in, b_gate, nat_rel_bias, lam_q1, lam_k1, lam_q2, lam_k2, diff_ln_g, w_branch, w_out, g_ffn2, w_ffn2_in, w_ffn2_out, g_final):
    L = DEPTH
    D = D_MODEL
    Bp, Bs = c_prompt.shape[0], c_sample.shape[0]
    rows = ((Bp + Bs + 7) // 8) * 8
    c_all = jnp.concatenate([c_prompt, c_sample, jnp.zeros((rows - Bp - Bs, D), F32)], axis=0)
    mods = _modulation(c_all, w_ada, b_ada).reshape(L, rows, N_MOD, D)

    P = {}
    ffn1 = [_prep_ffn(w_ffn1_in[l], w_ffn1_out[l]) for l in range(L)]
    ffn2 = [_prep_ffn(w_ffn2_in[l], w_ffn2_out[l]) for l in range(L)]
    P["wab1"] = [a for a, _ in ffn1]
    P["wo1"] = [b for _, b in ffn1]
    P["wab2"] = [a for a, _ in ffn2]
    P["wo2"] = [b for _, b in ffn2]
    P["w_in"] = [_prep_w_in(w_in[l]) for l in range(L)]
    P["b_gate"] = [b_gate[l].reshape(1, N_BRANCH * D) for l in range(L)]
    P["nat_bias"] = [_nat_bias_table(nat_rel_bias[l]) for l in range(L)]
    n_j = D // MERGE_CHUNK
    P["w_branch"] = [jnp.transpose(w_branch[l].astype(BF16).reshape(N_BRANCH, BRANCH_WIDTH, n_j, MERGE_CHUNK),
                                   (2, 0, 1, 3)) for l in range(L)]
    P["w_out"] = [w_out[l].astype(BF16).reshape(n_j, MERGE_CHUNK, D) for l in range(L)]
    for name, arr in (("g_ffn1", g_ffn1), ("g_mix", g_mix), ("g_ffn2", g_ffn2), ("lam_q1", lam_q1),
                      ("lam_k1", lam_k1), ("lam_q2", lam_q2), ("lam_k2", lam_k2), ("diff_ln_g", diff_ln_g)):
        P[name] = [arr[l].reshape(1, -1) for l in range(L)]
    P["g_final"] = g_final.reshape(1, D)

    outs = []
    for x, lo, hi in ((x_prompt, 0, Bp), (x_sample, Bp, Bp + Bs)):
        T = x.shape[1]
        tabs = {"rope": _rope_tables(T), "dft": _dft_tables(T)}
        outs.append(_trunk(x, mods[:, lo:hi], P, tabs))
    return tuple(outs)
```

```python
import functools
import math

import numpy as np
import jax
import jax.numpy as jnp
from jax import lax
from jax.experimental import pallas as pl
from jax.experimental.pallas import tpu as pltpu

F32 = jnp.float32
BF16 = jnp.bfloat16

D_MODEL = 2048
DEPTH = 2
GRID_W = 64
HEAD_DIM = 64
ROT_DIM = HEAD_DIM // 4
ROPE_THETA = 500000.0
EPS = 1e-6
HALF = 0.5
FNET_GROUP_DIM = 128
NAT_HEADS = 8
NAT_WIN_H = 8
NAT_WIN_W = 16
DIFF_HEADS = 4
DIL_DILATIONS = (1, 4, 16)
DIL_SIDE = 64
DIL_HEADS = 8
N_BRANCH = 4
BRANCH_WIDTH = 512
N_MOD = 9
D_FF = ((8 * D_MODEL // 3 + 63) // 64) * 64
MIX_COLS = 8192

LANES = 128
VMEM_LIMIT = 56 * 1024 * 1024
FF_CHUNK = 512
D_FF_PAD = ((D_FF + FF_CHUNK - 1) // FF_CHUNK) * FF_CHUNK
FFN_TILE = 512
TOKEN_TILE = 1024
MERGE_TILE = 256
PROJ_CHUNK = 1024
MXU_COLS = 256
NEG_INF = -1e30
LOG2E = math.log2(math.e)


def _cparams(sem):
    return pltpu.CompilerParams(dimension_semantics=sem, vmem_limit_bytes=VMEM_LIMIT)


def _dot(a, b):
    return jnp.dot(a, b, preferred_element_type=F32)


def _dot_nt(a, b):
    return lax.dot_general(a, b, (((1,), (1,)), ((), ())), preferred_element_type=F32)


def _norm_modulate(x, g, sh, sc):
    ms = jnp.mean(x * x, axis=-1, keepdims=True)
    y = x * lax.rsqrt(ms + EPS) * g
    return y * (1.0 + sc) + sh


def _mod_kernel(c_ref, w_ref, b_ref, o_ref):
    c = c_ref[...]
    s = c * jax.nn.sigmoid(c)
    o_ref[0] = _dot(s.astype(BF16), w_ref[0].astype(BF16)) + b_ref[0]


def _modulation(c_all, w_ada, b_ada):
    L, D, N = w_ada.shape
    R = c_all.shape[0]
    tn = 1152
    return pl.pallas_call(
        _mod_kernel,
        grid=(L, N // tn),
        in_specs=[
            pl.BlockSpec((R, D), lambda l, j: (0, 0)),
            pl.BlockSpec((1, D, tn), lambda l, j: (l, 0, j)),
            pl.BlockSpec((1, 1, tn), lambda l, j: (l, 0, j)),
        ],
        out_specs=pl.BlockSpec((1, R, tn), lambda l, j: (l, 0, j)),
        out_shape=jax.ShapeDtypeStruct((L, R, N), F32),
        compiler_params=_cparams(("arbitrary", "arbitrary")),
    )(c_all, w_ada, b_ada.reshape(L, 1, N))


def _ffn_kernel(x_ref, g_ref, sh_ref, sc_ref, gt_ref, wab0_ref, wab1_ref, wo0_ref, wo1_ref, gf_ref, o_ref,
                h_scr, acc_scr, *, final_norm, n_chunks):
    k = pl.program_id(2)

    def chunk(wab_ref, wo_ref):
        ab = _dot(h_scr[...], wab_ref[...])
        a = ab[:, :FF_CHUNK]
        b = ab[:, FF_CHUNK:]
        act = (a * jax.nn.sigmoid(a)) * b
        return _dot(act.astype(BF16), wo_ref[...])

    def pair():
        acc_scr[...] += chunk(wab0_ref, wo0_ref) + chunk(wab1_ref, wo1_ref)

    if n_chunks % 2 == 0:
        @pl.when(k == 0)
        def _():
            h_scr[...] = _norm_modulate(x_ref[0], g_ref[...], sh_ref[0], sc_ref[0]).astype(BF16)
            acc_scr[...] = jnp.zeros_like(acc_scr)

        pair()
    else:
        @pl.when(k == 0)
        def _():
            h_scr[...] = _norm_modulate(x_ref[0], g_ref[...], sh_ref[0], sc_ref[0]).astype(BF16)
            acc_scr[...] = chunk(wab0_ref, wo0_ref)

        pl.when(k > 0)(pair)

    @pl.when(k == pl.num_programs(2) - 1)
    def _():
        xn = x_ref[0] + (HALF * gt_ref[0]) * acc_scr[...]
        if final_norm:
            ms = jnp.mean(xn * xn, axis=-1, keepdims=True)
            xn = xn * lax.rsqrt(ms + EPS) * gf_ref[...]
        o_ref[0] = xn


def _ffn(x, g, sh, sc, gt, wab, wo, g_final, final_norm):
    B, T, D = x.shape
    tm = min(FFN_TILE, T)
    n_chunks = wo.shape[0] // FF_CHUNK
    nk = (n_chunks + 1) // 2
    vec = pl.BlockSpec((1, D), lambda b, i, k: (0, 0))
    per_b = pl.BlockSpec((1, 1, D), lambda b, i, k: (b, 0, 0))
    if n_chunks % 2 == 0:
        first = lambda k: 2 * k
        second = lambda k: 2 * k + 1
    else:
        first = lambda k: jnp.maximum(2 * k - 1, 0)
        second = lambda k: jnp.maximum(2 * k, 2)
    return pl.pallas_call(
        functools.partial(_ffn_kernel, final_norm=final_norm, n_chunks=n_chunks),
        grid=(B, T // tm, nk),
        in_specs=[
            pl.BlockSpec((1, tm, D), lambda b, i, k: (b, i, 0)),
            vec, per_b, per_b, per_b,
            pl.BlockSpec((D, 2 * FF_CHUNK), lambda b, i, k: (0, first(k))),
            pl.BlockSpec((D, 2 * FF_CHUNK), lambda b, i, k: (0, second(k))),
            pl.BlockSpec((FF_CHUNK, D), lambda b, i, k: (first(k), 0)),
            pl.BlockSpec((FF_CHUNK, D), lambda b, i, k: (second(k), 0)),
            vec,
        ],
        out_specs=pl.BlockSpec((1, tm, D), lambda b, i, k: (b, i, 0)),
        out_shape=jax.ShapeDtypeStruct((B, T, D), F32),
        scratch_shapes=[pltpu.VMEM((tm, D), BF16), pltpu.VMEM((tm, D), F32)],
        compiler_params=_cparams(("arbitrary", "arbitrary", "arbitrary")),
    )(x, g, sh, sc, gt, wab, wab, wo, wo, g_final)


GATE_STEPS = N_BRANCH * D_MODEL // PROJ_CHUNK
FIRST_GATE_STEP = 8
DIL_SECTION = {d: d * BRANCH_WIDTH for d in DIL_DILATIONS}


def _rope_apply(z, ra, rb):
    return z * ra + pltpu.roll(z, LANES // 2, 1) * rb


def _inproj_kernel(x_ref, g_ref, sh_ref, sc_ref, w_ref, bg_ref, ra_ref, rb_ref,
                   tok_o, d4_o, d16_o, f_o, nq_o, rest_o, gate_o, h_scr, z_scr):
    j = pl.program_id(2)
    tm = h_scr.shape[0]
    n_sub = PROJ_CHUNK // MXU_COLS

    @pl.when(j == 0)
    def _():
        h = _norm_modulate(x_ref[0], g_ref[...], sh_ref[0], sc_ref[0])
        h_scr[...] = h.astype(BF16)

    def zsub(c):
        return _dot(h_scr[...], w_ref[:, c * MXU_COLS:(c + 1) * MXU_COLS])

    def roped(c):
        z = zsub(c)
        ra = ra_ref[...]
        rb = rb_ref[...]
        return [_rope_apply(z[:, u * LANES:(u + 1) * LANES], ra, rb) for u in range(MXU_COLS // LANES)]

    sub_groups = MXU_COLS // LANES
    dil_groups = BRANCH_WIDTH // LANES

    def scatter_residues():
        for d, o_ref, g0 in ((4, d4_o, 0), (16, d16_o, dil_groups)):
            for r in range(d):
                for u in range(dil_groups):
                    rows = z_scr[g0 + u, pl.ds(r, tm // d, stride=d), :]
                    lo = r * BRANCH_WIDTH + u * LANES
                    o_ref[0, :, lo:lo + LANES] = rows.astype(BF16)

    @pl.when((j == 0) | (j == 2))
    def _():
        for c in range(n_sub):
            for u, r in enumerate(roped(c)):
                lo = c * MXU_COLS + u * LANES
                tok_o[0, :, lo:lo + LANES] = r.astype(BF16)

    @pl.when((j == 1) | (j == 3))
    def _():
        for c in range(n_sub):
            for u, r in enumerate(roped(c)):
                z_scr[c * sub_groups + u] = r
        scatter_residues()

    @pl.when(j == 7)
    def _():
        for c in range(n_sub):
            z = zsub(c)
            for u in range(sub_groups):
                z_scr[c * sub_groups + u] = z[:, u * LANES:(u + 1) * LANES]
        scatter_residues()

    @pl.when(j == 4)
    def _():
        for c in range(n_sub):
            o_ref, lo = (f_o, c * MXU_COLS) if c < n_sub // 2 else (nq_o, (c - n_sub // 2) * MXU_COLS)
            o_ref[0, :, lo:lo + MXU_COLS] = zsub(c).astype(BF16)

    @pl.when((j == 5) | (j == 6))
    def _():
        for c in range(n_sub):
            rest_o[0, :, c * MXU_COLS:(c + 1) * MXU_COLS] = zsub(c).astype(BF16)

    @pl.when(j >= FIRST_GATE_STEP)
    def _():
        for c in range(n_sub):
            cols = slice(c * MXU_COLS, (c + 1) * MXU_COLS)
            pre = zsub(c) + bg_ref[:, cols]
            gate_o[0, :, cols] = (0.5 * jnp.tanh(0.5 * pre) + 0.5).astype(BF16)


def _inproj(x, g, sh, sc, w, bg, ra, rb):
    B, T, D = x.shape
    tm = min(TOKEN_TILE, T)
    n_steps = w.shape[1] // PROJ_CHUNK
    vec = pl.BlockSpec((1, D), lambda b, i, j: (0, 0))
    per_b = pl.BlockSpec((1, 1, D), lambda b, i, j: (b, 0, 0))
    tab = pl.BlockSpec((tm, LANES), lambda b, i, j: (i, 0))
    step = lambda j, *edges: sum((j >= e).astype(jnp.int32) for e in edges)
    dil_spec = lambda d: pl.BlockSpec((1, tm // d, DIL_SECTION[d]), lambda b, i, j: (b, i, step(j, 3, 7)))
    tok512 = pl.BlockSpec((1, tm, BRANCH_WIDTH), lambda b, i, j: (b, i, 0))
    return pl.pallas_call(
        _inproj_kernel,
        grid=(B, T // tm, n_steps),
        in_specs=[
            pl.BlockSpec((1, tm, D), lambda b, i, j: (b, i, 0), pipeline_mode=pl.Buffered(1)),
            vec, per_b, per_b,
            pl.BlockSpec((D, PROJ_CHUNK), lambda b, i, j: (0, j)),
            pl.BlockSpec((1, PROJ_CHUNK), lambda b, i, j: (0, jnp.maximum(j - FIRST_GATE_STEP, 0))),
            tab, tab,
        ],
        out_specs=[
            pl.BlockSpec((1, tm, PROJ_CHUNK), lambda b, i, j: (b, i, step(j, 2))),
            dil_spec(4), dil_spec(16),
            tok512, tok512,
            pl.BlockSpec((1, tm, PROJ_CHUNK), lambda b, i, j: (b, i, step(j, 6))),
            pl.BlockSpec((1, tm, PROJ_CHUNK), lambda b, i, j: (b, i, jnp.maximum(j - FIRST_GATE_STEP, 0))),
        ],
        out_shape=[
            jax.ShapeDtypeStruct((B, T, 2 * PROJ_CHUNK), BF16),
            jax.ShapeDtypeStruct((B, T // 4, 3 * DIL_SECTION[4]), BF16),
            jax.ShapeDtypeStruct((B, T // 16, 3 * DIL_SECTION[16]), BF16),
            jax.ShapeDtypeStruct((B, T, BRANCH_WIDTH), BF16),
            jax.ShapeDtypeStruct((B, T, BRANCH_WIDTH), BF16),
            jax.ShapeDtypeStruct((B, T, 2 * PROJ_CHUNK), BF16),
            jax.ShapeDtypeStruct((B, T, GATE_STEPS * PROJ_CHUNK), BF16),
        ],
        scratch_shapes=[pltpu.VMEM((tm, D), BF16), pltpu.VMEM((PROJ_CHUNK // LANES, tm, LANES), F32)],
        compiler_params=_cparams(("arbitrary", "arbitrary", "arbitrary")),
    )(x, g, sh, sc, w, bg, ra, rb)


FFT_T2 = 128
FFT_COLS = 2048
FFT_K1_BLOCK = 4


def _fft1_kernel(u_ref, c1_ref, s1_ref, ar_ref, ai_ref):
    u = u_ref[0]
    ar_ref[0] = _dot(c1_ref[...], u).astype(BF16)
    ai_ref[0] = (-_dot(s1_ref[...], u)).astype(BF16)


def _fft2_kernel(ar_ref, ai_ref, twc_ref, tws_ref, c2_ref, s2_ref, cc_ref, sc_ref, o_ref, *, norm):
    c2 = c2_ref[...]
    s2 = s2_ref[...]
    cc = cc_ref[...]
    sc = sc_ref[...]
    for kk in range(FFT_K1_BLOCK):
        rows = slice(kk * FFT_T2, (kk + 1) * FFT_T2)
        ar = ar_ref[0, rows, :].astype(F32)
        ai = ai_ref[0, rows, :].astype(F32)
        twc = jnp.concatenate([twc_ref[rows, :]] * (BRANCH_WIDTH // LANES), axis=1)
        tws = jnp.concatenate([tws_ref[rows, :]] * (BRANCH_WIDTH // LANES), axis=1)
        br = (ar * twc + ai * tws).astype(BF16)
        bi = (ai * twc - ar * tws).astype(BF16)
        zr = _dot(c2, br) + _dot(s2, bi)
        zi = _dot(c2, bi) - _dot(s2, br)
        y = _dot(zr.astype(BF16), cc) + _dot(zi.astype(BF16), sc)
        o_ref[0, :, kk * BRANCH_WIDTH:(kk + 1) * BRANCH_WIDTH] = (y * norm).astype(BF16)


def _dft_tables(T):
    T1 = T // FFT_T2
    k1 = np.arange(T1)
    a1 = 2.0 * np.pi * np.outer(k1, k1) / T1
    k2 = np.arange(FFT_T2)
    a2 = 2.0 * np.pi * np.outer(k2, k2) / FFT_T2
    atw = 2.0 * np.pi * np.outer(k1, k2).reshape(T, 1) / T
    atw = np.broadcast_to(atw, (T, LANES))
    ch = np.arange(FNET_GROUP_DIM)
    ac = 2.0 * np.pi * np.outer(ch, ch) / FNET_GROUP_DIM
    eye = np.eye(BRANCH_WIDTH // FNET_GROUP_DIM)
    bf = lambda a: jnp.asarray(a, dtype=BF16)
    return dict(c1=bf(np.cos(a1)), s1=bf(np.sin(a1)), c2=bf(np.cos(a2)), s2=bf(np.sin(a2)),
                twc=jnp.asarray(np.cos(atw), F32), tws=jnp.asarray(np.sin(atw), F32),
                cc=bf(np.kron(eye, np.cos(ac))), sc=bf(np.kron(eye, np.sin(ac))))


def _fourier_mix(f, tabs):
    B, T, C = f.shape
    T1 = T // FFT_T2
    n_col = FFT_T2 * C // FFT_COLS
    full2 = lambda shape: pl.BlockSpec(shape, lambda b, i: (0, 0))
    ar, ai = pl.pallas_call(
        _fft1_kernel,
        grid=(B, n_col),
        in_specs=[pl.BlockSpec((1, T1, FFT_COLS), lambda b, i: (b, 0, i)),
                  full2((T1, T1)), full2((T1, T1))],
        out_specs=[pl.BlockSpec((1, T1, FFT_COLS), lambda b, i: (b, 0, i))] * 2,
        out_shape=[jax.ShapeDtypeStruct((B, T1, FFT_T2 * C), BF16)] * 2,
        compiler_params=_cparams(("arbitrary", "arbitrary")),
    )(f.reshape(B, T1, FFT_T2 * C), tabs["c1"], tabs["s1"])
    ar = ar.reshape(B, T, C)
    ai = ai.reshape(B, T, C)
    rows = FFT_K1_BLOCK * FFT_T2
    y = pl.pallas_call(
        functools.partial(_fft2_kernel, norm=1.0 / math.sqrt(T * FNET_GROUP_DIM)),
        grid=(B, T1 // FFT_K1_BLOCK),
        in_specs=[pl.BlockSpec((1, rows, C), lambda b, i: (b, i, 0)),
                  pl.BlockSpec((1, rows, C), lambda b, i: (b, i, 0)),
                  pl.BlockSpec((rows, LANES), lambda b, i: (i, 0)),
                  pl.BlockSpec((rows, LANES), lambda b, i: (i, 0)),
                  full2((FFT_T2, FFT_T2)), full2((FFT_T2, FFT_T2)),
                  full2((C, C)), full2((C, C))],
        out_specs=pl.BlockSpec((1, FFT_T2, FFT_K1_BLOCK * C), lambda b, i: (b, 0, i)),
        out_shape=jax.ShapeDtypeStruct((B, FFT_T2, T1 * C), BF16),
        compiler_params=_cparams(("arbitrary", "arbitrary")),
    )(ar, ai, tabs["twc"], tabs["tws"], tabs["c2"], tabs["s2"], tabs["cc"], tabs["sc"])
    return y.reshape(B, T, C)


NAT_Q_ROWS = 8
NAT_Q_TOK = NAT_Q_ROWS * GRID_W
NAT_EDGE_TOK = (NAT_WIN_H // 2) * GRID_W
NAT_WIN_TOK = NAT_Q_TOK + 2 * NAT_EDGE_TOK


def _head_masks():
    lane = lax.broadcasted_iota(jnp.int32, (1, LANES), 1)
    return lane < HEAD_DIM


def _nat_kernel(q_ref, kp_ref, kc_ref, kn_ref, vp_ref, vc_ref, vn_ref, bias_ref, o_ref, kw, vw):
    kw[0:NAT_EDGE_TOK, :] = kp_ref[0]
    kw[NAT_EDGE_TOK:NAT_EDGE_TOK + NAT_Q_TOK, :] = kc_ref[0]
    kw[NAT_EDGE_TOK + NAT_Q_TOK:, :] = kn_ref[0]
    vw[0:NAT_EDGE_TOK, :] = vp_ref[0]
    vw[NAT_EDGE_TOK:NAT_EDGE_TOK + NAT_Q_TOK, :] = vc_ref[0]
    vw[NAT_EDGE_TOK + NAT_Q_TOK:, :] = vn_ref[0]
    first = _head_masks()
    ones = jnp.ones((NAT_WIN_TOK, LANES), BF16)
    for hp in range(NAT_HEADS // 2):
        cols = slice(hp * LANES, (hp + 1) * LANES)
        q = q_ref[0, :, cols]
        k = kw[:, cols]
        v1 = jnp.concatenate([vw[:, cols], ones], axis=1)
        outs = []
        for hh in range(2):
            sel = first if hh == 0 else jnp.logical_not(first)
            qm = jnp.where(sel, q, jnp.zeros_like(q))
            s = _dot_nt(qm, k) + bias_ref[0, 2 * hp + hh]
            m = jnp.max(s, axis=-1, keepdims=True)
            ov = _dot(jnp.exp2(s - m).astype(BF16), v1)
            outs.append(ov[:, :LANES] / ov[:, LANES:])
        o_ref[0, :, cols] = jnp.where(first, outs[0], outs[1]).astype(BF16)


def _nat_bias_table(rel_bias):
    H = rel_bias.shape[0]
    half = NAT_WIN_H // 2
    col = np.arange(GRID_W)
    col_start = np.clip(col - NAT_WIN_W // 2, 0, GRID_W - NAT_WIN_W)
    kc = np.arange(GRID_W)
    col_ok = (kc[None, :] >= col_start[:, None]) & (kc[None, :] < col_start[:, None] + NAT_WIN_W)
    col_off = np.clip(kc[None, :] - col[:, None] + (NAT_WIN_W - 1), 0, 2 * NAT_WIN_W - 2)
    rr = np.arange(NAT_Q_ROWS)
    wr = np.arange(NAT_Q_ROWS + NAT_WIN_H)
    start = np.stack([np.maximum(rr - half, 0) + half, rr, np.minimum(rr, half)])
    row_ok = (wr[None, None, :] >= start[:, :, None]) & (wr[None, None, :] < start[:, :, None] + NAT_WIN_H)
    row_off = np.clip(wr[None, :] - half - rr[:, None] + (NAT_WIN_H - 1), 0, 2 * NAT_WIN_H - 2)
    tab = rel_bias[:, row_off][:, :, :, col_off]
    ok = row_ok[:, None, :, :, None, None] & col_ok[None, None, None, None]
    tab = jnp.where(jnp.asarray(ok), tab.astype(F32)[None] * LOG2E, NEG_INF)
    tab = jnp.transpose(tab, (0, 1, 2, 4, 3, 5))
    return tab.reshape(3, H, NAT_Q_TOK, NAT_WIN_TOK)


def _neighborhood_attention(nq, rest, bias_tab):
    B, T, C = nq.shape
    n_blk = T // NAT_Q_TOK
    kind = lambda i: 1 - (i == 0).astype(jnp.int32) + (i == n_blk - 1).astype(jnp.int32)
    per = NAT_Q_TOK // NAT_EDGE_TOK
    n_edge = T // NAT_EDGE_TOK
    prev = lambda c: (lambda b, i: (b, jnp.maximum(i * per - 1, 0), c))
    cur = lambda c: (lambda b, i: (b, i, c))
    nxt = lambda c: (lambda b, i: (b, jnp.minimum((i + 1) * per, n_edge - 1), c))
    edge = lambda f: pl.BlockSpec((1, NAT_EDGE_TOK, C), f)
    mid = lambda f: pl.BlockSpec((1, NAT_Q_TOK, C), f)
    return pl.pallas_call(
        _nat_kernel,
        grid=(B, n_blk),
        in_specs=[mid(cur(0)),
                  edge(prev(0)), mid(cur(0)), edge(nxt(0)),
                  edge(prev(1)), mid(cur(1)), edge(nxt(1)),
                  pl.BlockSpec((1,) + bias_tab.shape[1:], lambda b, i: (kind(i), 0, 0, 0),
                               pipeline_mode=pl.Buffered(1))],
        out_specs=mid(cur(0)),
        out_shape=jax.ShapeDtypeStruct((B, T, C), BF16),
        scratch_shapes=[pltpu.VMEM((NAT_WIN_TOK, C), BF16), pltpu.VMEM((NAT_WIN_TOK, C), BF16)],
        compiler_params=_cparams(("arbitrary", "arbitrary")),
    )(nq, rest, rest, rest, rest, rest, rest, bias_tab)


DIFF_TQ = 1024
DIFF_TK = 512
DIFF_UNROLL = 16
DIFF_HEADROOM = 64.0


def _qk_head_mask():
    lane = lax.broadcasted_iota(jnp.int32, (1, LANES), 1)
    return (lane % (LANES // 2)) < HEAD_DIM // 2


def _diff_kernel(q_ref, k_ref, v_ref, lq1_ref, lk1_ref, lq2_ref, lk2_ref, g_ref, o_ref, a1_scr, a2_scr,
                 *, lam_init, n_kv):
    first = _qk_head_mask()
    q = q_ref[0]
    zero = jnp.zeros_like(q)
    q1 = jnp.where(first, q, zero)
    q2 = jnp.where(first, zero, q)
    tq = q.shape[0]
    ones = jnp.ones((DIFF_TK, LANES), BF16)
    zacc = jnp.zeros((tq, 2 * LANES), F32)

    def tile(c):
        k0 = pl.multiple_of(c * DIFF_TK, DIFF_TK)
        k = k_ref[0, pl.ds(k0, DIFF_TK), :]
        v1 = jnp.concatenate([v_ref[0, pl.ds(k0, DIFF_TK), :], ones], axis=1)
        return k, v1

    k, _ = tile(0)
    r1 = jnp.max(_dot_nt(q1, k), axis=-1, keepdims=True)
    r2 = jnp.max(_dot_nt(q2, k), axis=-1, keepdims=True)

    def fast(c, carry):
        a1, a2, t1, t2 = carry
        k, v1 = tile(c)

        def one(qm, r, a, t):
            s = _dot_nt(qm, k)
            for u in range(DIFF_TK // LANES):
                t = jnp.maximum(t, s[:, u * LANES:(u + 1) * LANES])
            return a + _dot(jnp.exp2(s - r).astype(BF16), v1), t

        a1, t1 = one(q1, r1, a1, t1)
        a2, t2 = one(q2, r2, a2, t2)
        return a1, a2, t1, t2

    tneg = jnp.full((tq, LANES), NEG_INF, F32)
    a1, a2, t1, t2 = lax.fori_loop(0, n_kv, fast, (zacc, zacc, tneg, tneg), unroll=DIFF_UNROLL)
    a1_scr[...] = a1
    a2_scr[...] = a2
    growth = jnp.max(jnp.maximum(t1 - r1, t2 - r2))

    @pl.when(jnp.logical_not(growth <= DIFF_HEADROOM))
    def _():
        def slow(c, carry):
            m1, b1, m2, b2 = carry
            k, v1 = tile(c)

            def one(qm, m, a):
                s = _dot_nt(qm, k)
                mn = jnp.maximum(m, jnp.max(s, axis=-1, keepdims=True))
                return mn, jnp.exp2(m - mn) * a + _dot(jnp.exp2(s - mn).astype(BF16), v1)

            m1, b1 = one(q1, m1, b1)
            m2, b2 = one(q2, m2, b2)
            return m1, b1, m2, b2

        neg = jnp.full((tq, 1), NEG_INF, F32)
        _, b1, _, b2 = lax.fori_loop(0, n_kv, slow, (neg, zacc, neg, zacc))
        a1_scr[...] = b1
        a2_scr[...] = b2

    lam = (jnp.exp(jnp.sum(lq1_ref[...] * lk1_ref[...], keepdims=True))
           - jnp.exp(jnp.sum(lq2_ref[...] * lk2_ref[...], keepdims=True)) + lam_init)
    o = (a1_scr[:, :LANES] / a1_scr[:, LANES:]) - lam * (a2_scr[:, :LANES] / a2_scr[:, LANES:])
    ms = jnp.mean(o * o, axis=-1, keepdims=True)
    o = o * lax.rsqrt(ms + EPS) * g_ref[...]
    o_ref[0] = (o * (1.0 - lam_init)).astype(BF16)


def _diff_attention(tok, rest, lq1, lk1, lq2, lk2, ln_g, lam_init):
    B, T, _ = tok.shape
    tq = min(DIFF_TQ, T)
    k_blk = 1024 // LANES
    v_blk = 1024 // LANES
    vec = lambda n: pl.BlockSpec((1, n), lambda b, h, i: (0, 0))
    return pl.pallas_call(
        functools.partial(_diff_kernel, lam_init=lam_init, n_kv=T // DIFF_TK),
        grid=(B, DIFF_HEADS, T // tq),
        in_specs=[pl.BlockSpec((1, tq, LANES), lambda b, h, i: (b, i, h)),
                  pl.BlockSpec((1, T, LANES), lambda b, h, i: (b, 0, k_blk + h)),
                  pl.BlockSpec((1, T, LANES), lambda b, h, i: (b, 0, v_blk + h)),
                  vec(HEAD_DIM), vec(HEAD_DIM), vec(HEAD_DIM), vec(HEAD_DIM), vec(LANES)],
        out_specs=pl.BlockSpec((1, tq, LANES), lambda b, h, i: (b, i, h)),
        out_shape=jax.ShapeDtypeStruct((B, T, DIFF_HEADS * LANES), BF16),
        scratch_shapes=[pltpu.VMEM((tq, 2 * LANES), F32), pltpu.VMEM((tq, 2 * LANES), F32)],
        compiler_params=_cparams(("arbitrary", "arbitrary", "arbitrary")),
    )(tok, tok, rest, lq1, lk1, lq2, lk2, ln_g)


DIL_TQ = 256
DIL_EDGE = 128


def _dil_kernel(q_ref, kp_ref, kc_ref, kn_ref, vp_ref, vc_ref, vn_ref, o_ref, lse_ref, kw, vw, *, seq, tq):
    i = pl.program_id(2)
    kw[0:DIL_EDGE, :] = kp_ref[0]
    kw[DIL_EDGE:DIL_EDGE + tq, :] = kc_ref[0]
    kw[DIL_EDGE + tq:, :] = kn_ref[0]
    vw[0:DIL_EDGE, :] = vp_ref[0]
    vw[DIL_EDGE:DIL_EDGE + tq, :] = vc_ref[0]
    vw[DIL_EDGE + tq:, :] = vn_ref[0]
    win = tq + 2 * DIL_EDGE
    s0 = i * tq
    qpos = s0 + lax.broadcasted_iota(jnp.int32, (tq, win), 0)
    kpos = s0 - DIL_EDGE + lax.broadcasted_iota(jnp.int32, (tq, win), 1)
    valid = (jnp.abs(kpos - qpos) <= DIL_SIDE) & (kpos >= 0) & (kpos < seq)
    first_qk = _qk_head_mask()
    first_v = _head_masks()
    ones = jnp.ones((win, LANES), BF16)
    for hp in range(DIL_HEADS // 2):
        cols = slice(hp * LANES, (hp + 1) * LANES)
        q = q_ref[0, :, cols]
        k = kw[:, cols]
        v1 = jnp.concatenate([vw[:, cols], ones], axis=1)
        outs, lses = [], []
        for hh in range(2):
            sel = first_qk if hh == 0 else jnp.logical_not(first_qk)
            qm = jnp.where(sel, q, jnp.zeros_like(q))
            s = jnp.where(valid, _dot_nt(qm, k), NEG_INF)
            m = jnp.max(s, axis=-1, keepdims=True)
            ov = _dot(jnp.exp2(s - m).astype(BF16), v1)
            l = ov[:, LANES:]
            outs.append(ov[:, :LANES] / l)
            lses.append(m + jnp.log2(l))
        o_ref[0, :, cols] = jnp.where(first_v, outs[0], outs[1]).astype(BF16)
        lse_ref[0, :, cols] = jnp.where(first_v, lses[0], lses[1])


def _dilated_group(qa, ka, va, qc, kc, vc, dil):
    B, seq, _ = qa.shape
    C = DIL_HEADS * HEAD_DIM
    tq = min(DIL_TQ, seq)
    per = tq // DIL_EDGE
    n_edge = seq // DIL_EDGE
    prev = lambda cf: (lambda b, r, i: (b, jnp.maximum(i * per - 1, 0), cf(r)))
    cur = lambda cf: (lambda b, r, i: (b, i, cf(r)))
    nxt = lambda cf: (lambda b, r, i: (b, jnp.minimum((i + 1) * per, n_edge - 1), cf(r)))
    edge = lambda f: pl.BlockSpec((1, DIL_EDGE, C), f)
    mid = lambda f: pl.BlockSpec((1, tq, C), f)
    out_map = lambda b, r, i: (b, i, r)
    return pl.pallas_call(
        functools.partial(_dil_kernel, seq=seq, tq=tq),
        grid=(B, dil, seq // tq),
        in_specs=[mid(cur(qc)),
                  edge(prev(kc)), mid(cur(kc)), edge(nxt(kc)),
                  edge(prev(vc)), mid(cur(vc)), edge(nxt(vc))],
        out_specs=[pl.BlockSpec((1, tq, C), out_map), pl.BlockSpec((1, tq, C), out_map)],
        out_shape=[jax.ShapeDtypeStruct((B, seq, dil * C), BF16),
                   jax.ShapeDtypeStruct((B, seq, dil * C), F32)],
        scratch_shapes=[pltpu.VMEM((tq + 2 * DIL_EDGE, C), BF16), pltpu.VMEM((tq + 2 * DIL_EDGE, C), BF16)],
        compiler_params=_cparams(("arbitrary", "arbitrary", "arbitrary")),
    )(qa, ka, ka, ka, va, va, va)


MERGE_CHUNK = 512


def _merge_kernel(x_ref, gt_ref, ya_ref, yb_ref, yc_ref, o0_ref, o1_ref, o2_ref, s0_ref, s1_ref, s2_ref,
                  g_ref, wb_ref, wo_ref, out_ref, o1_scr, s1_scr, o2_scr, s2_scr):
    tm = out_ref.shape[1]
    C = BRANCH_WIDTH
    D = out_ref.shape[2]
    n_j = D // MERGE_CHUNK
    for d, o_ref, s_ref, o_scr, s_scr in ((4, o1_ref, s1_ref, o1_scr, s1_scr),
                                          (16, o2_ref, s2_ref, o2_scr, s2_scr)):
        for r in range(d):
            for u in range(C // LANES):
                cols = slice(r * C + u * LANES, r * C + (u + 1) * LANES)
                o_scr[u, pl.ds(r, tm // d, stride=d), :] = o_ref[0, :, cols].astype(F32)
                s_scr[u, pl.ds(r, tm // d, stride=d), :] = s_ref[0, :, cols]
    wide = lambda scr: jnp.concatenate([scr[u] for u in range(C // LANES)], axis=1)
    s0 = s0_ref[0]
    s1 = wide(s1_scr)
    s2 = wide(s2_scr)
    m = jnp.maximum(jnp.maximum(s0, s1), s2)
    e0 = jnp.exp2(s0 - m)
    e1 = jnp.exp2(s1 - m)
    e2 = jnp.exp2(s2 - m)
    num = o0_ref[0].astype(F32) * e0 + wide(o1_scr) * e1 + wide(o2_scr) * e2
    ys = [ya_ref[0], yb_ref[0], yc_ref[0], (num / (e0 + e1 + e2)).astype(BF16)]

    acc = None
    for j in range(n_j):
        merged = None
        for n in range(N_BRANCH):
            lo = n * D + j * MERGE_CHUNK
            term = g_ref[0, :, lo:lo + MERGE_CHUNK].astype(F32) * _dot(ys[n], wb_ref[j, n])
            merged = term if merged is None else merged + term
        part = _dot(merged.astype(BF16), wo_ref[j])
        acc = part if acc is None else acc + part
    out_ref[0] = x_ref[0] + gt_ref[0] * acc


def _merge(x, gt, ya, yb, yc, dil_o, dil_lse, gates, wb, wo):
    B, T, D = x.shape
    tm = min(MERGE_TILE, T)
    C = BRANCH_WIDTH
    n_j = D // MERGE_CHUNK
    tok = pl.BlockSpec((1, tm, C), lambda b, i: (b, i, 0))
    res = lambda d: pl.BlockSpec((1, tm // d, d * C), lambda b, i: (b, i, 0))
    dil_specs = [res(d) for d in DIL_DILATIONS]
    return pl.pallas_call(
        _merge_kernel,
        grid=(B, T // tm),
        in_specs=[pl.BlockSpec((1, tm, D), lambda b, i: (b, i, 0)),
                  pl.BlockSpec((1, 1, D), lambda b, i: (b, 0, 0)),
                  tok, tok, tok, *dil_specs, *dil_specs,
                  pl.BlockSpec((1, tm, N_BRANCH * D), lambda b, i: (b, i, 0)),
                  pl.BlockSpec((n_j, N_BRANCH, C, MERGE_CHUNK), lambda b, i: (0, 0, 0, 0),
                               pipeline_mode=pl.Buffered(1)),
                  pl.BlockSpec((n_j, MERGE_CHUNK, D), lambda b, i: (0, 0, 0),
                               pipeline_mode=pl.Buffered(1))],
        out_specs=pl.BlockSpec((1, tm, D), lambda b, i: (b, i, 0)),
        out_shape=jax.ShapeDtypeStruct((B, T, D), F32),
        scratch_shapes=[pltpu.VMEM((C // LANES, tm, LANES), F32)] * 4,
        compiler_params=_cparams(("arbitrary", "arbitrary")),
    )(x, gt, ya, yb, yc, *dil_o, *dil_lse, gates, wb, wo)


def _prep_ffn(w_in, w_out):
    D = w_in.shape[0]
    pad = D_FF_PAD - D_FF
    a = jnp.pad(w_in[:, :D_FF].astype(BF16), ((0, 0), (0, pad)))
    b = jnp.pad(w_in[:, D_FF:].astype(BF16), ((0, 0), (0, pad)))
    n = D_FF_PAD // FF_CHUNK
    wab = jnp.concatenate([a.reshape(D, n, FF_CHUNK), b.reshape(D, n, FF_CHUNK)], axis=2)
    wo = jnp.pad(w_out.astype(BF16), ((0, pad), (0, 0)))
    return wab.reshape(D, n * 2 * FF_CHUNK), wo


def _qk_lane_order():
    rot_half = ROT_DIM // 2
    rest_half = (HEAD_DIM - ROT_DIM) // 2
    idx = []
    for n in range(LANES):
        half, w = divmod(n, LANES // 2)
        head, i = divmod(w, HEAD_DIM // 2)
        d = i + rot_half * half if i < rot_half else ROT_DIM + (i - rot_half) + rest_half * half
        idx.append(head * HEAD_DIM + d)
    return np.asarray(idx)


def _prep_w_in(w):
    W = BRANCH_WIDTH
    D = w.shape[0]
    qscale = HEAD_DIM ** -0.5 * LOG2E
    order = _qk_lane_order()
    qk = lambda cols: cols.reshape(D, -1, LANES)[:, :, order].reshape(D, -1)
    f_in, nq, nk, nv, dq, dk, dv = [w[:, n * W:(n + 1) * W] for n in range(7)]
    lq = [qk(w[:, (7 + g) * W:(8 + g) * W] * qscale) for g in range(3)]
    lk = [qk(w[:, (10 + g) * W:(11 + g) * W]) for g in range(3)]
    lv = [w[:, (13 + g) * W:(14 + g) * W] for g in range(3)]
    gates = w[:, MIX_COLS:]
    cols = [qk(dq * qscale), lq[0], lq[1], lq[2], qk(dk), lk[0], lk[1], lk[2],
            f_in, nq * qscale, nk, nv, dv, lv[0], lv[1], lv[2], gates]
    return jnp.concatenate(cols, axis=1).astype(BF16)


def _rope_tables(T):
    rot_half = ROT_DIM // 2
    inv = ROPE_THETA ** (-jnp.arange(0, ROT_DIM, 2, dtype=F32) / ROT_DIM)
    ang = jnp.arange(T, dtype=F32)[:, None] * inv[None, :]
    cos, sin = jnp.cos(ang), jnp.sin(ang)
    lane = np.arange(LANES)
    i = lane % (HEAD_DIM // 2)
    is_rot = jnp.asarray(i < rot_half)[None, :]
    src = np.minimum(i, rot_half - 1)
    sign = jnp.asarray(np.where(lane < LANES // 2, -1.0, 1.0), F32)[None, :]
    ra = jnp.where(is_rot, cos[:, src], 1.0)
    rb = jnp.where(is_rot, sin[:, src] * sign, 0.0)
    return ra, rb


def _trunk(x, mods, P, tabs):
    B, T, D = x.shape
    for l in range(DEPTH):
        sh1, sc1, gt1, sh2, sc2, gt2, sh3, sc3, gt3 = [mods[l][:, i:i + 1, :] for i in range(N_MOD)]
        x = _ffn(x, P["g_ffn1"][l], sh1, sc1, gt1, P["wab1"][l], P["wo1"][l], P["g_final"], False)
        tok, d4, d16, f_in, nq, rest, gates = _inproj(x, P["g_mix"][l], sh2, sc2, P["w_in"][l],
                                                      P["b_gate"][l], *tabs["rope"])
        ya = _fourier_mix(f_in, tabs["dft"])
        yb = _neighborhood_attention(nq, rest, P["nat_bias"][l])
        lam_init = 0.8 - 0.6 * math.exp(-0.3 * l)
        yc = _diff_attention(tok, rest, P["lam_q1"][l], P["lam_k1"][l], P["lam_q2"][l], P["lam_k2"][l],
                             P["diff_ln_g"][l], lam_init)
        dil = [_dilated_group(tok, tok, rest, lambda r: 1, lambda r: 3, lambda r: 3, 1),
               _dilated_group(d4, d4, d4, lambda r: r, lambda r: 4 + r, lambda r: 8 + r, 4),
               _dilated_group(d16, d16, d16, lambda r: r, lambda r: 16 + r, lambda r: 32 + r, 16)]
        x = _merge(x, gt2, ya, yb, yc, [o for o, _ in dil], [s for _, s in dil], gates,
                   P["w_branch"][l], P["w_out"][l])
        x = _ffn(x, P["g_ffn2"][l], sh3, sc3, gt3, P["wab2"][l], P["wo2"][l], P["g_final"], l == DEPTH - 1)
    return x


def kernel(x_prompt, x_sample, c_prompt, c_sample, w_ada, b_ada, g_ffn1, w_ffn1_in, w_ffn1_out, g_mix, w_in, b_gate, nat_rel_bias, lam_q1, lam_k1, lam_q2, lam_k2, diff_ln_g, w_branch, w_out, g_ffn2, w_ffn2_in, w_ffn2_out, g_final):
    L = DEPTH
    D = D_MODEL
    Bp, Bs = c_prompt.shape[0], c_sample.shape[0]
    rows = ((Bp + Bs + 7) // 8) * 8
    c_all = jnp.concatenate([c_prompt, c_sample, jnp.zeros((rows - Bp - Bs, D), F32)], axis=0)
    mods = _modulation(c_all, w_ada, b_ada).reshape(L, rows, N_MOD, D)

    P = {}
    ffn1 = [_prep_ffn(w_ffn1_in[l], w_ffn1_out[l]) for l in range(L)]
    ffn2 = [_prep_ffn(w_ffn2_in[l], w_ffn2_out[l]) for l in range(L)]
    P["wab1"] = [a for a, _ in ffn1]
    P["wo1"] = [b for _, b in ffn1]
    P["wab2"] = [a for a, _ in ffn2]
    P["wo2"] = [b for _, b in ffn2]
    P["w_in"] = [_prep_w_in(w_in[l]) for l in range(L)]
    P["b_gate"] = [b_gate[l].reshape(1, N_BRANCH * D) for l in range(L)]
    P["nat_bias"] = [_nat_bias_table(nat_rel_bias[l]) for l in range(L)]
    n_j = D // MERGE_CHUNK
    P["w_branch"] = [jnp.transpose(w_branch[l].astype(BF16).reshape(N_BRANCH, BRANCH_WIDTH, n_j, MERGE_CHUNK),
                                   (2, 0, 1, 3)) for l in range(L)]
    P["w_out"] = [w_out[l].astype(BF16).reshape(n_j, MERGE_CHUNK, D) for l in range(L)]
    for name, arr in (("g_ffn1", g_ffn1), ("g_mix", g_mix), ("g_ffn2", g_ffn2), ("lam_q1", lam_q1),
                      ("lam_k1", lam_k1), ("lam_q2", lam_q2), ("lam_k2", lam_k2), ("diff_ln_g", diff_ln_g)):
        P[name] = [arr[l].reshape(1, -1) for l in range(L)]
    P["g_final"] = g_final.reshape(1, D)

    outs = []
    for x, lo, hi in ((x_prompt, 0, Bp), (x_sample, Bp, Bp + Bs)):
        T = x.shape[1]
        tabs = {"rope": _rope_tables(T), "dft": _dft_tables(T)}
        outs.append(_trunk(x, mods[:, lo:hi], P, tabs))
    return tuple(outs)
```

```python
import functools
import math

import numpy as np
import jax
import jax.numpy as jnp
from jax import lax
from jax.experimental import pallas as pl
from jax.experimental.pallas import tpu as pltpu

F32 = jnp.float32
BF16 = jnp.bfloat16

D_MODEL = 2048
DEPTH = 2
GRID_W = 64
HEAD_DIM = 64
ROT_DIM = HEAD_DIM // 4
ROPE_THETA = 500000.0
EPS = 1e-6
HALF = 0.5
FNET_GROUP_DIM = 128
NAT_HEADS = 8
NAT_WIN_H = 8
NAT_WIN_W = 16
DIFF_HEADS = 4
DIL_DILATIONS = (1, 4, 16)
DIL_SIDE = 64
DIL_HEADS = 8
N_BRANCH = 4
BRANCH_WIDTH = 512
N_MOD = 9
D_FF = ((8 * D_MODEL // 3 + 63) // 64) * 64
MIX_COLS = 8192

LANES = 128
VMEM_LIMIT = 56 * 1024 * 1024
FF_CHUNK = 512
D_FF_PAD = ((D_FF + FF_CHUNK - 1) // FF_CHUNK) * FF_CHUNK
FFN_TILE = 512
TOKEN_TILE = 1024
MERGE_TILE = 256
PROJ_CHUNK = 1024
MXU_COLS = 256
NEG_INF = -1e30
LOG2E = math.log2(math.e)


def _cparams(sem):
    return pltpu.CompilerParams(dimension_semantics=sem, vmem_limit_bytes=VMEM_LIMIT)


def _dot(a, b):
    return jnp.dot(a, b, preferred_element_type=F32)


def _dot_nt(a, b):
    return lax.dot_general(a, b, (((1,), (1,)), ((), ())), preferred_element_type=F32)


def _norm_modulate(x, g, sh, sc):
    ms = jnp.mean(x * x, axis=-1, keepdims=True)
    y = x * lax.rsqrt(ms + EPS) * g
    return y * (1.0 + sc) + sh


def _mod_kernel(c_ref, w_ref, b_ref, o_ref):
    c = c_ref[...]
    s = c * jax.nn.sigmoid(c)
    o_ref[0] = _dot(s.astype(BF16), w_ref[0].astype(BF16)) + b_ref[0]


def _modulation(c_all, w_ada, b_ada):
    L, D, N = w_ada.shape
    R = c_all.shape[0]
    tn = 1152
    return pl.pallas_call(
        _mod_kernel,
        grid=(L, N // tn),
        in_specs=[
            pl.BlockSpec((R, D), lambda l, j: (0, 0)),
            pl.BlockSpec((1, D, tn), lambda l, j: (l, 0, j)),
            pl.BlockSpec((1, 1, tn), lambda l, j: (l, 0, j)),
        ],
        out_specs=pl.BlockSpec((1, R, tn), lambda l, j: (l, 0, j)),
        out_shape=jax.ShapeDtypeStruct((L, R, N), F32),
        compiler_params=_cparams(("arbitrary", "arbitrary")),
    )(c_all, w_ada, b_ada.reshape(L, 1, N))


def _ffn_kernel(x_ref, g_ref, sh_ref, sc_ref, gt_ref, wab0_ref, wab1_ref, wo0_ref, wo1_ref, gf_ref, o_ref,
                h_scr, acc_scr, *, final_norm, n_chunks):
    k = pl.program_id(2)

    def chunk(wab_ref, wo_ref):
        ab = _dot(h_scr[...], wab_ref[...])
        a = ab[:, :FF_CHUNK]
        b = ab[:, FF_CHUNK:]
        act = (a * jax.nn.sigmoid(a)) * b
        return _dot(act.astype(BF16), wo_ref[...])

    def pair():
        acc_scr[...] += chunk(wab0_ref, wo0_ref) + chunk(wab1_ref, wo1_ref)

    if n_chunks % 2 == 0:
        @pl.when(k == 0)
        def _():
            h_scr[...] = _norm_modulate(x_ref[0], g_ref[...], sh_ref[0], sc_ref[0]).astype(BF16)
            acc_scr[...] = jnp.zeros_like(acc_scr)

        pair()
    else:
        @pl.when(k == 0)
        def _():
            h_scr[...] = _norm_modulate(x_ref[0], g_ref[...], sh_ref[0], sc_ref[0]).astype(BF16)
            acc_scr[...] = chunk(wab0_ref, wo0_ref)

        pl.when(k > 0)(pair)

    @pl.when(k == pl.num_programs(2) - 1)
    def _():
        xn = x_ref[0] + (HALF * gt_ref[0]) * acc_scr[...]
        if final_norm:
            ms = jnp.mean(xn * xn, axis=-1, keepdims=True)
            xn = xn * lax.rsqrt(ms + EPS) * gf_ref[...]
        o_ref[0] = xn


def _ffn(x, g, sh, sc, gt, wab, wo, g_final, final_norm):
    B, T, D = x.shape
    tm = min(FFN_TILE, T)
    n_chunks = wo.shape[0] // FF_CHUNK
    nk = (n_chunks + 1) // 2
    vec = pl.BlockSpec((1, D), lambda b, i, k: (0, 0))
    per_b = pl.BlockSpec((1, 1, D), lambda b, i, k: (b, 0, 0))
    if n_chunks % 2 == 0:
        first = lambda k: 2 * k
        second = lambda k: 2 * k + 1
    else:
        first = lambda k: jnp.maximum(2 * k - 1, 0)
        second = lambda k: jnp.maximum(2 * k, 2)
    return pl.pallas_call(
        functools.partial(_ffn_kernel, final_norm=final_norm, n_chunks=n_chunks),
        grid=(B, T // tm, nk),
        in_specs=[
            pl.BlockSpec((1, tm, D), lambda b, i, k: (b, i, 0)),
            vec, per_b, per_b, per_b,
            pl.BlockSpec((D, 2 * FF_CHUNK), lambda b, i, k: (0, first(k))),
            pl.BlockSpec((D, 2 * FF_CHUNK), lambda b, i, k: (0, second(k))),
            pl.BlockSpec((FF_CHUNK, D), lambda b, i, k: (first(k), 0)),
            pl.BlockSpec((FF_CHUNK, D), lambda b, i, k: (second(k), 0)),
            vec,
        ],
        out_specs=pl.BlockSpec((1, tm, D), lambda b, i, k: (b, i, 0)),
        out_shape=jax.ShapeDtypeStruct((B, T, D), F32),
        scratch_shapes=[pltpu.VMEM((tm, D), BF16), pltpu.VMEM((tm, D), F32)],
        compiler_params=_cparams(("arbitrary", "arbitrary", "arbitrary")),
    )(x, g, sh, sc, gt, wab, wab, wo, wo, g_final)


GATE_STEPS = N_BRANCH * D_MODEL // PROJ_CHUNK
FIRST_GATE_STEP = 8
DIL_SECTION = {d: d * BRANCH_WIDTH for d in DIL_DILATIONS}


def _rope_apply(z, ra, rb):
    return z * ra + pltpu.roll(z, LANES // 2, 1) * rb


def _inproj_kernel(x_ref, g_ref, sh_ref, sc_ref, w_ref, bg_ref, ra_ref, rb_ref,
                   tok_o, d4_o, d16_o, f_o, nq_o, rest_o, gate_o, h_scr, z4_scr, z16_scr):
    j = pl.program_id(2)
    tm = h_scr.shape[0]
    n_sub = PROJ_CHUNK // MXU_COLS

    def zsub(c):
        return _dot(h_scr[...], w_ref[:, c * MXU_COLS:(c + 1) * MXU_COLS])

    def roped(c):
        z = zsub(c)
        ra = ra_ref[...]
        rb = rb_ref[...]
        return [_rope_apply(z[:, u * LANES:(u + 1) * LANES], ra, rb) for u in range(MXU_COLS // LANES)]

    sub_groups = MXU_COLS // LANES
    dil_groups = BRANCH_WIDTH // LANES

    def tok_step():
        for c in range(n_sub):
            for u, r in enumerate(roped(c)):
                lo = c * MXU_COLS + u * LANES
                tok_o[0, :, lo:lo + LANES] = r.astype(BF16)

    @pl.when(j == 0)
    def _():
        h_scr[...] = _norm_modulate(x_ref[0], g_ref[...], sh_ref[0], sc_ref[0]).astype(BF16)
        tok_step()

    pl.when(j == 2)(tok_step)

    def dil_step(rope):
        for c in range(n_sub):
            parts = roped(c) if rope else [zsub(c)[:, u * LANES:(u + 1) * LANES] for u in range(sub_groups)]
            scr = z4_scr if c < n_sub // 2 else z16_scr
            for u, r in enumerate(parts):
                scr[(c % (n_sub // 2)) * sub_groups + u] = r
        for d, o_ref, scr in ((4, d4_o, z4_scr), (16, d16_o, z16_scr)):
            for r in range(d):
                for u in range(dil_groups):
                    rows = scr[u, pl.ds(r, tm // d, stride=d), :]
                    lo = r * BRANCH_WIDTH + u * LANES
                    o_ref[0, :, lo:lo + LANES] = rows.astype(BF16)

    pl.when((j == 1) | (j == 3))(lambda: dil_step(True))
    pl.when(j == 7)(lambda: dil_step(False))

    @pl.when(j == 4)
    def _():
        for c in range(n_sub):
            o_ref, lo = (f_o, c * MXU_COLS) if c < n_sub // 2 else (nq_o, (c - n_sub // 2) * MXU_COLS)
            o_ref[0, :, lo:lo + MXU_COLS] = zsub(c).astype(BF16)

    @pl.when((j == 5) | (j == 6))
    def _():
        for c in range(n_sub):
            rest_o[0, :, c * MXU_COLS:(c + 1) * MXU_COLS] = zsub(c).astype(BF16)

    @pl.when(j >= FIRST_GATE_STEP)
    def _():
        for c in range(n_sub):
            cols = slice(c * MXU_COLS, (c + 1) * MXU_COLS)
            pre = zsub(c) + bg_ref[:, cols]
            gate_o[0, :, cols] = (0.5 * jnp.tanh(0.5 * pre) + 0.5).astype(BF16)


def _inproj(x, g, sh, sc, w, bg, ra, rb):
    B, T, D = x.shape
    tm = min(TOKEN_TILE, T)
    n_steps = w.shape[1] // PROJ_CHUNK
    vec = pl.BlockSpec((1, D), lambda b, i, j: (0, 0))
    per_b = pl.BlockSpec((1, 1, D), lambda b, i, j: (b, 0, 0))
    tab = pl.BlockSpec((tm, LANES), lambda b, i, j: (i, 0))
    step = lambda j, *edges: sum((j >= e).astype(jnp.int32) for e in edges)
    dil_spec = lambda d: pl.BlockSpec((1, tm // d, DIL_SECTION[d]), lambda b, i, j: (b, i, step(j, 3, 7)))
    tok512 = pl.BlockSpec((1, tm, BRANCH_WIDTH), lambda b, i, j: (b, i, 0))
    return pl.pallas_call(
        _inproj_kernel,
        grid=(B, T // tm, n_steps),
        in_specs=[
            pl.BlockSpec((1, tm, D), lambda b, i, j: (b, i, 0), pipeline_mode=pl.Buffered(1)),
            vec, per_b, per_b,
            pl.BlockSpec((D, PROJ_CHUNK), lambda b, i, j: (0, j)),
            pl.BlockSpec((1, PROJ_CHUNK), lambda b, i, j: (0, jnp.maximum(j - FIRST_GATE_STEP, 0))),
            tab, tab,
        ],
        out_specs=[
            pl.BlockSpec((1, tm, PROJ_CHUNK), lambda b, i, j: (b, i, step(j, 2))),
            dil_spec(4), dil_spec(16),
            tok512, tok512,
            pl.BlockSpec((1, tm, PROJ_CHUNK), lambda b, i, j: (b, i, step(j, 6))),
            pl.BlockSpec((1, tm, PROJ_CHUNK), lambda b, i, j: (b, i, jnp.maximum(j - FIRST_GATE_STEP, 0))),
        ],
        out_shape=[
            jax.ShapeDtypeStruct((B, T, 2 * PROJ_CHUNK), BF16),
            jax.ShapeDtypeStruct((B, T // 4, 3 * DIL_SECTION[4]), BF16),
            jax.ShapeDtypeStruct((B, T // 16, 3 * DIL_SECTION[16]), BF16),
            jax.ShapeDtypeStruct((B, T, BRANCH_WIDTH), BF16),
            jax.ShapeDtypeStruct((B, T, BRANCH_WIDTH), BF16),
            jax.ShapeDtypeStruct((B, T, 2 * PROJ_CHUNK), BF16),
            jax.ShapeDtypeStruct((B, T, GATE_STEPS * PROJ_CHUNK), BF16),
        ],
        scratch_shapes=[pltpu.VMEM((tm, D), BF16)] + [pltpu.VMEM((BRANCH_WIDTH // LANES, tm, LANES), F32)] * 2,
        compiler_params=_cparams(("arbitrary", "arbitrary", "arbitrary")),
    )(x, g, sh, sc, w, bg, ra, rb)


FFT_T2 = 128
FFT_COLS = 2048
FFT_K1_BLOCK = 4


def _fft1_kernel(u_ref, c1_ref, s1_ref, ar_ref, ai_ref):
    u = u_ref[0]
    ar_ref[0] = _dot(c1_ref[...], u).astype(BF16)
    ai_ref[0] = (-_dot(s1_ref[...], u)).astype(BF16)


def _fft2_kernel(ar_ref, ai_ref, twc_ref, tws_ref, c2_ref, s2_ref, cc_ref, sc_ref, o_ref, *, norm):
    c2 = c2_ref[...]
    s2 = s2_ref[...]
    cc = cc_ref[...]
    sc = sc_ref[...]
    for kk in range(FFT_K1_BLOCK):
        rows = slice(kk * FFT_T2, (kk + 1) * FFT_T2)
        ar = ar_ref[0, rows, :].astype(F32)
        ai = ai_ref[0, rows, :].astype(F32)
        twc = jnp.concatenate([twc_ref[rows, :]] * (BRANCH_WIDTH // LANES), axis=1)
        tws = jnp.concatenate([tws_ref[rows, :]] * (BRANCH_WIDTH // LANES), axis=1)
        br = (ar * twc + ai * tws).astype(BF16)
        bi = (ai * twc - ar * tws).astype(BF16)
        zr = _dot(c2, br) + _dot(s2, bi)
        zi = _dot(c2, bi) - _dot(s2, br)
        y = _dot(zr.astype(BF16), cc) + _dot(zi.astype(BF16), sc)
        o_ref[0, :, kk * BRANCH_WIDTH:(kk + 1) * BRANCH_WIDTH] = (y * norm).astype(BF16)


def _dft_tables(T):
    T1 = T // FFT_T2
    k1 = np.arange(T1)
    a1 = 2.0 * np.pi * np.outer(k1, k1) / T1
    k2 = np.arange(FFT_T2)
    a2 = 2.0 * np.pi * np.outer(k2, k2) / FFT_T2
    atw = 2.0 * np.pi * np.outer(k1, k2).reshape(T, 1) / T
    atw = np.broadcast_to(atw, (T, LANES))
    ch = np.arange(FNET_GROUP_DIM)
    ac = 2.0 * np.pi * np.outer(ch, ch) / FNET_GROUP_DIM
    eye = np.eye(BRANCH_WIDTH // FNET_GROUP_DIM)
    bf = lambda a: jnp.asarray(a, dtype=BF16)
    return dict(c1=bf(np.cos(a1)), s1=bf(np.sin(a1)), c2=bf(np.cos(a2)), s2=bf(np.sin(a2)),
                twc=jnp.asarray(np.cos(atw), F32), tws=jnp.asarray(np.sin(atw), F32),
                cc=bf(np.kron(eye, np.cos(ac))), sc=bf(np.kron(eye, np.sin(ac))))


def _fourier_mix(f, tabs):
    B, T, C = f.shape
    T1 = T // FFT_T2
    n_col = FFT_T2 * C // FFT_COLS
    full2 = lambda shape: pl.BlockSpec(shape, lambda b, i: (0, 0))
    ar, ai = pl.pallas_call(
        _fft1_kernel,
        grid=(B, n_col),
        in_specs=[pl.BlockSpec((1, T1, FFT_COLS), lambda b, i: (b, 0, i)),
                  full2((T1, T1)), full2((T1, T1))],
        out_specs=[pl.BlockSpec((1, T1, FFT_COLS), lambda b, i: (b, 0, i))] * 2,
        out_shape=[jax.ShapeDtypeStruct((B, T1, FFT_T2 * C), BF16)] * 2,
        compiler_params=_cparams(("arbitrary", "arbitrary")),
    )(f.reshape(B, T1, FFT_T2 * C), tabs["c1"], tabs["s1"])
    ar = ar.reshape(B, T, C)
    ai = ai.reshape(B, T, C)
    rows = FFT_K1_BLOCK * FFT_T2
    y = pl.pallas_call(
        functools.partial(_fft2_kernel, norm=1.0 / math.sqrt(T * FNET_GROUP_DIM)),
        grid=(B, T1 // FFT_K1_BLOCK),
        in_specs=[pl.BlockSpec((1, rows, C), lambda b, i: (b, i, 0)),
                  pl.BlockSpec((1, rows, C), lambda b, i: (b, i, 0)),
                  pl.BlockSpec((rows, LANES), lambda b, i: (i, 0)),
                  pl.BlockSpec((rows, LANES), lambda b, i: (i, 0)),
                  full2((FFT_T2, FFT_T2)), full2((FFT_T2, FFT_T2)),
                  full2((C, C)), full2((C, C))],
        out_specs=pl.BlockSpec((1, FFT_T2, FFT_K1_BLOCK * C), lambda b, i: (b, 0, i)),
        out_shape=jax.ShapeDtypeStruct((B, FFT_T2, T1 * C), BF16),
        compiler_params=_cparams(("arbitrary", "arbitrary")),
    )(ar, ai, tabs["twc"], tabs["tws"], tabs["c2"], tabs["s2"], tabs["cc"], tabs["sc"])
    return y.reshape(B, T, C)


NAT_Q_ROWS = 8
NAT_Q_TOK = NAT_Q_ROWS * GRID_W
NAT_EDGE_TOK = (NAT_WIN_H // 2) * GRID_W
NAT_WIN_TOK = NAT_Q_TOK + 2 * NAT_EDGE_TOK


def _head_masks():
    lane = lax.broadcasted_iota(jnp.int32, (1, LANES), 1)
    return lane < HEAD_DIM


def _nat_kernel(q_ref, kp_ref, kc_ref, kn_ref, vp_ref, vc_ref, vn_ref, bias_ref, o_ref, kw, vw):
    kw[0:NAT_EDGE_TOK, :] = kp_ref[0]
    kw[NAT_EDGE_TOK:NAT_EDGE_TOK + NAT_Q_TOK, :] = kc_ref[0]
    kw[NAT_EDGE_TOK + NAT_Q_TOK:, :] = kn_ref[0]
    vw[0:NAT_EDGE_TOK, :] = vp_ref[0]
    vw[NAT_EDGE_TOK:NAT_EDGE_TOK + NAT_Q_TOK, :] = vc_ref[0]
    vw[NAT_EDGE_TOK + NAT_Q_TOK:, :] = vn_ref[0]
    first = _head_masks()
    ones = jnp.ones((NAT_WIN_TOK, LANES), BF16)
    for hp in range(NAT_HEADS // 2):
        cols = slice(hp * LANES, (hp + 1) * LANES)
        q = q_ref[0, :, cols]
        k = kw[:, cols]
        v1 = jnp.concatenate([vw[:, cols], ones], axis=1)
        outs = []
        for hh in range(2):
            sel = first if hh == 0 else jnp.logical_not(first)
            qm = jnp.where(sel, q, jnp.zeros_like(q))
            s = _dot_nt(qm, k) + bias_ref[0, 2 * hp + hh]
            m = jnp.max(s, axis=-1, keepdims=True)
            ov = _dot(jnp.exp2(s - m).astype(BF16), v1)
            outs.append(ov[:, :LANES] / ov[:, LANES:])
        o_ref[0, :, cols] = jnp.where(first, outs[0], outs[1]).astype(BF16)


def _nat_bias_table(rel_bias):
    H = rel_bias.shape[0]
    half = NAT_WIN_H // 2
    col = np.arange(GRID_W)
    col_start = np.clip(col - NAT_WIN_W // 2, 0, GRID_W - NAT_WIN_W)
    kc = np.arange(GRID_W)
    col_ok = (kc[None, :] >= col_start[:, None]) & (kc[None, :] < col_start[:, None] + NAT_WIN_W)
    col_off = np.clip(kc[None, :] - col[:, None] + (NAT_WIN_W - 1), 0, 2 * NAT_WIN_W - 2)
    rr = np.arange(NAT_Q_ROWS)
    wr = np.arange(NAT_Q_ROWS + NAT_WIN_H)
    start = np.stack([np.maximum(rr - half, 0) + half, rr, np.minimum(rr, half)])
    row_ok = (wr[None, None, :] >= start[:, :, None]) & (wr[None, None, :] < start[:, :, None] + NAT_WIN_H)
    row_off = np.clip(wr[None, :] - half - rr[:, None] + (NAT_WIN_H - 1), 0, 2 * NAT_WIN_H - 2)
    tab = rel_bias[:, row_off][:, :, :, col_off]
    ok = row_ok[:, None, :, :, None, None] & col_ok[None, None, None, None]
    tab = jnp.where(jnp.asarray(ok), tab.astype(F32)[None] * LOG2E, NEG_INF)
    tab = jnp.transpose(tab, (0, 1, 2, 4, 3, 5))
    return tab.reshape(3, H, NAT_Q_TOK, NAT_WIN_TOK)


def _neighborhood_attention(nq, rest, bias_tab):
    B, T, C = nq.shape
    n_blk = T // NAT_Q_TOK
    kind = lambda i: 1 - (i == 0).astype(jnp.int32) + (i == n_blk - 1).astype(jnp.int32)
    per = NAT_Q_TOK // NAT_EDGE_TOK
    n_edge = T // NAT_EDGE_TOK
    prev = lambda c: (lambda b, i: (b, jnp.maximum(i * per - 1, 0), c))
    cur = lambda c: (lambda b, i: (b, i, c))
    nxt = lambda c: (lambda b, i: (b, jnp.minimum((i + 1) * per, n_edge - 1), c))
    edge = lambda f: pl.BlockSpec((1, NAT_EDGE_TOK, C), f)
    mid = lambda f: pl.BlockSpec((1, NAT_Q_TOK, C), f)
    return pl.pallas_call(
        _nat_kernel,
        grid=(B, n_blk),
        in_specs=[mid(cur(0)),
                  edge(prev(0)), mid(cur(0)), edge(nxt(0)),
                  edge(prev(1)), mid(cur(1)), edge(nxt(1)),
                  pl.BlockSpec((1,) + bias_tab.shape[1:], lambda b, i: (kind(i), 0, 0, 0),
                               pipeline_mode=pl.Buffered(1))],
        out_specs=mid(cur(0)),
        out_shape=jax.ShapeDtypeStruct((B, T, C), BF16),
        scratch_shapes=[pltpu.VMEM((NAT_WIN_TOK, C), BF16), pltpu.VMEM((NAT_WIN_TOK, C), BF16)],
        compiler_params=_cparams(("arbitrary", "arbitrary")),
    )(nq, rest, rest, rest, rest, rest, rest, bias_tab)


DIFF_TQ = 1024
DIFF_TK = 512
DIFF_UNROLL = 16
DIFF_HEADROOM = 64.0


def _qk_head_mask():
    lane = lax.broadcasted_iota(jnp.int32, (1, LANES), 1)
    return (lane % (LANES // 2)) < HEAD_DIM // 2


def _diff_kernel(q_ref, k_ref, v_ref, lq1_ref, lk1_ref, lq2_ref, lk2_ref, g_ref, o_ref, a1_scr, a2_scr,
                 *, lam_init, n_kv):
    first = _qk_head_mask()
    q = q_ref[0]
    zero = jnp.zeros_like(q)
    q1 = jnp.where(first, q, zero)
    q2 = jnp.where(first, zero, q)
    tq = q.shape[0]
    ones = jnp.ones((DIFF_TK, LANES), BF16)
    zacc = jnp.zeros((tq, 2 * LANES), F32)

    def tile(c):
        k0 = pl.multiple_of(c * DIFF_TK, DIFF_TK)
        k = k_ref[0, pl.ds(k0, DIFF_TK), :]
        v1 = jnp.concatenate([v_ref[0, pl.ds(k0, DIFF_TK), :], ones], axis=1)
        return k, v1

    k, _ = tile(0)
    r1 = jnp.max(_dot_nt(q1, k), axis=-1, keepdims=True)
    r2 = jnp.max(_dot_nt(q2, k), axis=-1, keepdims=True)

    def fast(c, carry):
        a1, a2, t1, t2 = carry
        k, v1 = tile(c)

        def one(qm, r, a, t):
            s = _dot_nt(qm, k)
            for u in range(DIFF_TK // LANES):
                t = jnp.maximum(t, s[:, u * LANES:(u + 1) * LANES])
            return a + _dot(jnp.exp2(s - r).astype(BF16), v1), t

        a1, t1 = one(q1, r1, a1, t1)
        a2, t2 = one(q2, r2, a2, t2)
        return a1, a2, t1, t2

    tneg = jnp.full((tq, LANES), NEG_INF, F32)
    a1, a2, t1, t2 = lax.fori_loop(0, n_kv, fast, (zacc, zacc, tneg, tneg), unroll=DIFF_UNROLL)
    a1_scr[...] = a1
    a2_scr[...] = a2
    growth = jnp.max(jnp.maximum(t1 - r1, t2 - r2))

    @pl.when(jnp.logical_not(growth <= DIFF_HEADROOM))
    def _():
        def slow(c, carry):
            m1, b1, m2, b2 = carry
            k, v1 = tile(c)

            def one(qm, m, a):
                s = _dot_nt(qm, k)
                mn = jnp.maximum(m, jnp.max(s, axis=-1, keepdims=True))
                return mn, jnp.exp2(m - mn) * a + _dot(jnp.exp2(s - mn).astype(BF16), v1)

            m1, b1 = one(q1, m1, b1)
            m2, b2 = one(q2, m2, b2)
            return m1, b1, m2, b2

        neg = jnp.full((tq, 1), NEG_INF, F32)
        _, b1, _, b2 = lax.fori_loop(0, n_kv, slow, (neg, zacc, neg, zacc))
        a1_scr[...] = b1
        a2_scr[...] = b2

    lam = (jnp.exp(jnp.sum(lq1_ref[...] * lk1_ref[...], keepdims=True))
           - jnp.exp(jnp.sum(lq2_ref[...] * lk2_ref[...], keepdims=True)) + lam_init)
    o = (a1_scr[:, :LANES] / a1_scr[:, LANES:]) - lam * (a2_scr[:, :LANES] / a2_scr[:, LANES:])
    ms = jnp.mean(o * o, axis=-1, keepdims=True)
    o = o * lax.rsqrt(ms + EPS) * g_ref[...]
    o_ref[0] = (o * (1.0 - lam_init)).astype(BF16)


def _diff_attention(tok, rest, lq1, lk1, lq2, lk2, ln_g, lam_init):
    B, T, _ = tok.shape
    tq = min(DIFF_TQ, T)
    k_blk = 1024 // LANES
    v_blk = 1024 // LANES
    vec = lambda n: pl.BlockSpec((1, n), lambda b, h, i: (0, 0))
    return pl.pallas_call(
        functools.partial(_diff_kernel, lam_init=lam_init, n_kv=T // DIFF_TK),
        grid=(B, DIFF_HEADS, T // tq),
        in_specs=[pl.BlockSpec((1, tq, LANES), lambda b, h, i: (b, i, h)),
                  pl.BlockSpec((1, T, LANES), lambda b, h, i: (b, 0, k_blk + h)),
                  pl.BlockSpec((1, T, LANES), lambda b, h, i: (b, 0, v_blk + h)),
                  vec(HEAD_DIM), vec(HEAD_DIM), vec(HEAD_DIM), vec(HEAD_DIM), vec(LANES)],
        out_specs=pl.BlockSpec((1, tq, LANES), lambda b, h, i: (b, i, h)),
        out_shape=jax.ShapeDtypeStruct((B, T, DIFF_HEADS * LANES), BF16),
        scratch_shapes=[pltpu.VMEM((tq, 2 * LANES), F32), pltpu.VMEM((tq, 2 * LANES), F32)],
        compiler_params=_cparams(("arbitrary", "arbitrary", "arbitrary")),
    )(tok, tok, rest, lq1, lk1, lq2, lk2, ln_g)


DIL_TQ = 1024
DIL_SUB = 256
DIL_EDGE = 128


def _dil_kernel(q_ref, kp_ref, kc_ref, kn_ref, vp_ref, vc_ref, vn_ref, o_ref, lse_ref, kw, vw, *, seq, tq):
    i = pl.program_id(2)
    kw[0:DIL_EDGE, :] = kp_ref[0]
    kw[DIL_EDGE:DIL_EDGE + tq, :] = kc_ref[0]
    kw[DIL_EDGE + tq:, :] = kn_ref[0]
    vw[0:DIL_EDGE, :] = vp_ref[0]
    vw[DIL_EDGE:DIL_EDGE + tq, :] = vc_ref[0]
    vw[DIL_EDGE + tq:, :] = vn_ref[0]
    sub = min(DIL_SUB, tq)
    win = sub + 2 * DIL_EDGE
    first_qk = _qk_head_mask()
    first_v = _head_masks()
    ones = jnp.ones((win, LANES), BF16)
    for t in range(tq // sub):
        s0 = i * tq + t * sub
        qpos = s0 + lax.broadcasted_iota(jnp.int32, (sub, win), 0)
        kpos = s0 - DIL_EDGE + lax.broadcasted_iota(jnp.int32, (sub, win), 1)
        valid = (jnp.abs(kpos - qpos) <= DIL_SIDE) & (kpos >= 0) & (kpos < seq)
        rows = slice(t * sub, (t + 1) * sub)
        wrows = slice(t * sub, t * sub + win)
        for hp in range(DIL_HEADS // 2):
            cols = slice(hp * LANES, (hp + 1) * LANES)
            q = q_ref[0, rows, cols]
            k = kw[wrows, cols]
            v1 = jnp.concatenate([vw[wrows, cols], ones], axis=1)
            outs, lses = [], []
            for hh in range(2):
                sel = first_qk if hh == 0 else jnp.logical_not(first_qk)
                qm = jnp.where(sel, q, jnp.zeros_like(q))
                s = jnp.where(valid, _dot_nt(qm, k), NEG_INF)
                m = jnp.max(s, axis=-1, keepdims=True)
                ov = _dot(jnp.exp2(s - m).astype(BF16), v1)
                l = ov[:, LANES:]
                outs.append(ov[:, :LANES] / l)
                lses.append(m + jnp.log2(l))
            o_ref[0, rows, cols] = jnp.where(first_v, outs[0], outs[1]).astype(BF16)
            lse_ref[0, rows, cols] = jnp.where(first_v, lses[0], lses[1])


def _dilated_group(qa, ka, va, qc, kc, vc, dil):
    B, seq, _ = qa.shape
    C = DIL_HEADS * HEAD_DIM
    tq = min(DIL_TQ, seq)
    per = tq // DIL_EDGE
    n_edge = seq // DIL_EDGE
    prev = lambda cf: (lambda b, r, i: (b, jnp.maximum(i * per - 1, 0), cf(r)))
    cur = lambda cf: (lambda b, r, i: (b, i, cf(r)))
    nxt = lambda cf: (lambda b, r, i: (b, jnp.minimum((i + 1) * per, n_edge - 1), cf(r)))
    edge = lambda f: pl.BlockSpec((1, DIL_EDGE, C), f)
    mid = lambda f: pl.BlockSpec((1, tq, C), f)
    out_map = lambda b, r, i: (b, i, r)
    return pl.pallas_call(
        functools.partial(_dil_kernel, seq=seq, tq=tq),
        grid=(B, dil, seq // tq),
        in_specs=[mid(cur(qc)),
                  edge(prev(kc)), mid(cur(kc)), edge(nxt(kc)),
                  edge(prev(vc)), mid(cur(vc)), edge(nxt(vc))],
        out_specs=[pl.BlockSpec((1, tq, C), out_map), pl.BlockSpec((1, tq, C), out_map)],
        out_shape=[jax.ShapeDtypeStruct((B, seq, dil * C), BF16),
                   jax.ShapeDtypeStruct((B, seq, dil * C), F32)],
        scratch_shapes=[pltpu.VMEM((tq + 2 * DIL_EDGE, C), BF16), pltpu.VMEM((tq + 2 * DIL_EDGE, C), BF16)],
        compiler_params=_cparams(("arbitrary", "arbitrary", "arbitrary")),
    )(qa, ka, ka, ka, va, va, va)


MERGE_CHUNK = 512


def _merge_kernel(x_ref, gt_ref, ya_ref, yb_ref, yc_ref, o0_ref, o1_ref, o2_ref, s0_ref, s1_ref, s2_ref,
                  g_ref, wb_ref, wo_ref, out_ref, o1_scr, s1_scr, o2_scr, s2_scr):
    tm = out_ref.shape[1]
    C = BRANCH_WIDTH
    D = out_ref.shape[2]
    n_j = D // MERGE_CHUNK
    for d, o_ref, s_ref, o_scr, s_scr in ((4, o1_ref, s1_ref, o1_scr, s1_scr),
                                          (16, o2_ref, s2_ref, o2_scr, s2_scr)):
        for r in range(d):
            for u in range(C // LANES):
                cols = slice(r * C + u * LANES, r * C + (u + 1) * LANES)
                o_scr[u, pl.ds(r, tm // d, stride=d), :] = o_ref[0, :, cols].astype(F32)
                s_scr[u, pl.ds(r, tm // d, stride=d), :] = s_ref[0, :, cols]
    wide = lambda scr: jnp.concatenate([scr[u] for u in range(C // LANES)], axis=1)
    s0 = s0_ref[0]
    s1 = wide(s1_scr)
    s2 = wide(s2_scr)
    m = jnp.maximum(jnp.maximum(s0, s1), s2)
    e0 = jnp.exp2(s0 - m)
    e1 = jnp.exp2(s1 - m)
    e2 = jnp.exp2(s2 - m)
    num = o0_ref[0].astype(F32) * e0 + wide(o1_scr) * e1 + wide(o2_scr) * e2
    ys = [ya_ref[0], yb_ref[0], yc_ref[0], (num / (e0 + e1 + e2)).astype(BF16)]

    acc = None
    for j in range(n_j):
        merged = None
        for n in range(N_BRANCH):
            lo = n * D + j * MERGE_CHUNK
            term = g_ref[0, :, lo:lo + MERGE_CHUNK].astype(F32) * _dot(ys[n], wb_ref[j, n])
            merged = term if merged is None else merged + term
        part = _dot(merged.astype(BF16), wo_ref[j])
        acc = part if acc is None else acc + part
    out_ref[0] = x_ref[0] + gt_ref[0] * acc


def _merge(x, gt, ya, yb, yc, dil_o, dil_lse, gates, wb, wo):
    B, T, D = x.shape
    tm = min(MERGE_TILE, T)
    C = BRANCH_WIDTH
    n_j = D // MERGE_CHUNK
    tok = pl.BlockSpec((1, tm, C), lambda b, i: (b, i, 0))
    res = lambda d: pl.BlockSpec((1, tm // d, d * C), lambda b, i: (b, i, 0))
    dil_specs = [res(d) for d in DIL_DILATIONS]
    return pl.pallas_call(
        _merge_kernel,
        grid=(B, T // tm),
        in_specs=[pl.BlockSpec((1, tm, D), lambda b, i: (b, i, 0)),
                  pl.BlockSpec((1, 1, D), lambda b, i: (b, 0, 0)),
                  tok, tok, tok, *dil_specs, *dil_specs,
                  pl.BlockSpec((1, tm, N_BRANCH * D), lambda b, i: (b, i, 0)),
                  pl.BlockSpec((n_j, N_BRANCH, C, MERGE_CHUNK), lambda b, i: (0, 0, 0, 0),
                               pipeline_mode=pl.Buffered(1)),
                  pl.BlockSpec((n_j, MERGE_CHUNK, D), lambda b, i: (0, 0, 0),
                               pipeline_mode=pl.Buffered(1))],
        out_specs=pl.BlockSpec((1, tm, D), lambda b, i: (b, i, 0)),
        out_shape=jax.ShapeDtypeStruct((B, T, D), F32),
        scratch_shapes=[pltpu.VMEM((C // LANES, tm, LANES), F32)] * 4,
        compiler_params=_cparams(("arbitrary", "arbitrary")),
    )(x, gt, ya, yb, yc, *dil_o, *dil_lse, gates, wb, wo)


def _prep_ffn(w_in, w_out):
    D = w_in.shape[0]
    pad = D_FF_PAD - D_FF
    a = jnp.pad(w_in[:, :D_FF].astype(BF16), ((0, 0), (0, pad)))
    b = jnp.pad(w_in[:, D_FF:].astype(BF16), ((0, 0), (0, pad)))
    n = D_FF_PAD // FF_CHUNK
    wab = jnp.concatenate([a.reshape(D, n, FF_CHUNK), b.reshape(D, n, FF_CHUNK)], axis=2)
    wo = jnp.pad(w_out.astype(BF16), ((0, pad), (0, 0)))
    return wab.reshape(D, n * 2 * FF_CHUNK), wo


def _qk_lane_order():
    rot_half = ROT_DIM // 2
    rest_half = (HEAD_DIM - ROT_DIM) // 2
    idx = []
    for n in range(LANES):
        half, w = divmod(n, LANES // 2)
        head, i = divmod(w, HEAD_DIM // 2)
        d = i + rot_half * half if i < rot_half else ROT_DIM + (i - rot_half) + rest_half * half
        idx.append(head * HEAD_DIM + d)
    return np.asarray(idx)


def _prep_w_in(w):
    W = BRANCH_WIDTH
    D = w.shape[0]
    qscale = HEAD_DIM ** -0.5 * LOG2E
    order = _qk_lane_order()
    qk = lambda cols: cols.reshape(D, -1, LANES)[:, :, order].reshape(D, -1)
    f_in, nq, nk, nv, dq, dk, dv = [w[:, n * W:(n + 1) * W] for n in range(7)]
    lq = [qk(w[:, (7 + g) * W:(8 + g) * W] * qscale) for g in range(3)]
    lk = [qk(w[:, (10 + g) * W:(11 + g) * W]) for g in range(3)]
    lv = [w[:, (13 + g) * W:(14 + g) * W] for g in range(3)]
    gates = w[:, MIX_COLS:]
    cols = [qk(dq * qscale), lq[0], lq[1], lq[2], qk(dk), lk[0], lk[1], lk[2],
            f_in, nq * qscale, nk, nv, dv, lv[0], lv[1], lv[2], gates]
    return jnp.concatenate(cols, axis=1).astype(BF16)


def _rope_tables(T):
    rot_half = ROT_DIM // 2
    inv = ROPE_THETA ** (-jnp.arange(0, ROT_DIM, 2, dtype=F32) / ROT_DIM)
    ang = jnp.arange(T, dtype=F32)[:, None] * inv[None, :]
    cos, sin = jnp.cos(ang), jnp.sin(ang)
    lane = np.arange(LANES)
    i = lane % (HEAD_DIM // 2)
    is_rot = jnp.asarray(i < rot_half)[None, :]
    src = np.minimum(i, rot_half - 1)
    sign = jnp.asarray(np.where(lane < LANES // 2, -1.0, 1.0), F32)[None, :]
    ra = jnp.where(is_rot, cos[:, src], 1.0)
    rb = jnp.where(is_rot, sin[:, src] * sign, 0.0)
    return ra, rb


def _trunk(x, mods, P, tabs):
    B, T, D = x.shape
    for l in range(DEPTH):
        sh1, sc1, gt1, sh2, sc2, gt2, sh3, sc3, gt3 = [mods[l][:, i:i + 1, :] for i in range(N_MOD)]
        x = _ffn(x, P["g_ffn1"][l], sh1, sc1, gt1, P["wab1"][l], P["wo1"][l], P["g_final"], False)
        tok, d4, d16, f_in, nq, rest, gates = _inproj(x, P["g_mix"][l], sh2, sc2, P["w_in"][l],
                                                      P["b_gate"][l], *tabs["rope"])
        ya = _fourier_mix(f_in, tabs["dft"])
        yb = _neighborhood_attention(nq, rest, P["nat_bias"][l])
        lam_init = 0.8 - 0.6 * math.exp(-0.3 * l)
        yc = _diff_attention(tok, rest, P["lam_q1"][l], P["lam_k1"][l], P["lam_q2"][l], P["lam_k2"][l],
                             P["diff_ln_g"][l], lam_init)
        dil = [_dilated_group(tok, tok, rest, lambda r: 1, lambda r: 3, lambda r: 3, 1),
               _dilated_group(d4, d4, d4, lambda r: r, lambda r: 4 + r, lambda r: 8 + r, 4),
               _dilated_group(d16, d16, d16, lambda r: r, lambda r: 16 + r, lambda r: 32 + r, 16)]
        x = _merge(x, gt2, ya, yb, yc, [o for o, _ in dil], [s for _, s in dil], gates,
                   P["w_branch"][l], P["w_out"][l])
        x = _ffn(x, P["g_ffn2"][l], sh3, sc3, gt3, P["wab2"][l], P["wo2"][l], P["g_final"], l == DEPTH - 1)
    return x


def kernel(x_prompt, x_sample, c_prompt, c_sample, w_ada, b_ada, g_ffn1, w_ffn1_in, w_ffn1_out, g_mix, w_in, b_gate, nat_rel_bias, lam_q1, lam_k1, lam_q2, lam_k2, diff_ln_g, w_branch, w_out, g_ffn2, w_ffn2_in, w_ffn2_out, g_final):
    L = DEPTH
    D = D_MODEL
    Bp, Bs = c_prompt.shape[0], c_sample.shape[0]
    rows = ((Bp + Bs + 7) // 8) * 8
    c_all = jnp.concatenate([c_prompt, c_sample, jnp.zeros((rows - Bp - Bs, D), F32)], axis=0)
    mods = _modulation(c_all, w_ada, b_ada).reshape(L, rows, N_MOD, D)

    P = {}
    ffn1 = [_prep_ffn(w_ffn1_in[l], w_ffn1_out[l]) for l in range(L)]
    ffn2 = [_prep_ffn(w_ffn2_in[l], w_ffn2_out[l]) for l in range(L)]
    P["wab1"] = [a for a, _ in ffn1]
    P["wo1"] = [b for _, b in ffn1]
    P["wab2"] = [a for a, _ in ffn2]
    P["wo2"] = [b for _, b in ffn2]
    P["w_in"] = [_prep_w_in(w_in[l]) for l in range(L)]
    P["b_gate"] = [b_gate[l].reshape(1, N_BRANCH * D) for l in range(L)]
    P["nat_bias"] = [_nat_bias_table(nat_rel_bias[l]) for l in range(L)]
    n_j = D // MERGE_CHUNK
    P["w_branch"] = [jnp.transpose(w_branch[l].astype(BF16).reshape(N_BRANCH, BRANCH_WIDTH, n_j, MERGE_CHUNK),
                                   (2, 0, 1, 3)) for l in range(L)]
    P["w_out"] = [w_out[l].astype(BF16).reshape(n_j, MERGE_CHUNK, D) for l in range(L)]
    for name, arr in (("g_ffn1", g_ffn1), ("g_mix", g_mix), ("g_ffn2", g_ffn2), ("lam_q1", lam_q1),
                      ("lam_k1", lam_k1), ("lam_q2", lam_q2), ("lam_k2", lam_k2), ("diff_ln_g", diff_ln_g)):
        P[name] = [arr[l].reshape(1, -1) for l in range(L)]
    P["g_final"] = g_final.reshape(1, D)

    outs = []
    for x, lo, hi in ((x_prompt, 0, Bp), (x_sample, Bp, Bp + Bs)):
        T = x.shape[1]
        tabs = {"rope": _rope_tables(T), "dft": _dft_tables(T)}
        outs.append(_trunk(x, mods[:, lo:hi], P, tabs))
    return tuple(outs)
```

```python
import functools
import math

import numpy as np
import jax
import jax.numpy as jnp
from jax import lax
from jax.experimental import pallas as pl
from jax.experimental.pallas import tpu as pltpu

F32 = jnp.float32
BF16 = jnp.bfloat16

D_MODEL = 2048
DEPTH = 2
GRID_W = 64
HEAD_DIM = 64
ROT_DIM = HEAD_DIM // 4
ROPE_THETA = 500000.0
EPS = 1e-6
HALF = 0.5
FNET_GROUP_DIM = 128
NAT_HEADS = 8
NAT_WIN_H = 8
NAT_WIN_W = 16
DIFF_HEADS = 4
DIL_DILATIONS = (1, 4, 16)
DIL_SIDE = 64
DIL_HEADS = 8
N_BRANCH = 4
BRANCH_WIDTH = 512
N_MOD = 9
D_FF = ((8 * D_MODEL // 3 + 63) // 64) * 64
MIX_COLS = 8192

LANES = 128
VMEM_LIMIT = 56 * 1024 * 1024
FF_CHUNK = 512
D_FF_PAD = ((D_FF + FF_CHUNK - 1) // FF_CHUNK) * FF_CHUNK
FFN_TILE = 512
TOKEN_TILE = 1024
MERGE_TILE = 256
PROJ_CHUNK = 1024
MXU_COLS = 256
NEG_INF = -1e30
LOG2E = math.log2(math.e)


def _cparams(sem):
    return pltpu.CompilerParams(dimension_semantics=sem, vmem_limit_bytes=VMEM_LIMIT)


def _dot(a, b):
    return jnp.dot(a, b, preferred_element_type=F32)


def _dot_nt(a, b):
    return lax.dot_general(a, b, (((1,), (1,)), ((), ())), preferred_element_type=F32)


def _norm_modulate(x, g, sh, sc):
    ms = jnp.mean(x * x, axis=-1, keepdims=True)
    y = x * lax.rsqrt(ms + EPS) * g
    return y * (1.0 + sc) + sh


def _mod_kernel(c_ref, w_ref, b_ref, o_ref):
    c = c_ref[...]
    s = c * jax.nn.sigmoid(c)
    o_ref[0] = _dot(s.astype(BF16), w_ref[0].astype(BF16)) + b_ref[0]


def _modulation(c_all, w_ada, b_ada):
    L, D, N = w_ada.shape
    R = c_all.shape[0]
    tn = 1152
    return pl.pallas_call(
        _mod_kernel,
        grid=(L, N // tn),
        in_specs=[
            pl.BlockSpec((R, D), lambda l, j: (0, 0)),
            pl.BlockSpec((1, D, tn), lambda l, j: (l, 0, j)),
            pl.BlockSpec((1, 1, tn), lambda l, j: (l, 0, j)),
        ],
        out_specs=pl.BlockSpec((1, R, tn), lambda l, j: (l, 0, j)),
        out_shape=jax.ShapeDtypeStruct((L, R, N), F32),
        compiler_params=_cparams(("arbitrary", "arbitrary")),
    )(c_all, w_ada, b_ada.reshape(L, 1, N))


def _ffn_kernel(x_ref, g_ref, sh_ref, sc_ref, gt_ref, wab0_ref, wab1_ref, wo0_ref, wo1_ref, gf_ref, o_ref,
                h_scr, acc_scr, *, final_norm, n_chunks):
    k = pl.program_id(2)

    def chunk(wab_ref, wo_ref):
        ab = _dot(h_scr[...], wab_ref[...])
        a = ab[:, :FF_CHUNK]
        b = ab[:, FF_CHUNK:]
        act = (a * jax.nn.sigmoid(a)) * b
        return _dot(act.astype(BF16), wo_ref[...])

    def pair():
        acc_scr[...] += chunk(wab0_ref, wo0_ref) + chunk(wab1_ref, wo1_ref)

    if n_chunks % 2 == 0:
        @pl.when(k == 0)
        def _():
            h_scr[...] = _norm_modulate(x_ref[0], g_ref[...], sh_ref[0], sc_ref[0]).astype(BF16)
            acc_scr[...] = jnp.zeros_like(acc_scr)

        pair()
    else:
        @pl.when(k == 0)
        def _():
            h_scr[...] = _norm_modulate(x_ref[0], g_ref[...], sh_ref[0], sc_ref[0]).astype(BF16)
            acc_scr[...] = chunk(wab0_ref, wo0_ref)

        pl.when(k > 0)(pair)

    @pl.when(k == pl.num_programs(2) - 1)
    def _():
        xn = x_ref[0] + (HALF * gt_ref[0]) * acc_scr[...]
        if final_norm:
            ms = jnp.mean(xn * xn, axis=-1, keepdims=True)
            xn = xn * lax.rsqrt(ms + EPS) * gf_ref[...]
        o_ref[0] = xn


def _ffn(x, g, sh, sc, gt, wab, wo, g_final, final_norm):
    B, T, D = x.shape
    tm = min(FFN_TILE, T)
    n_chunks = wo.shape[0] // FF_CHUNK
    nk = (n_chunks + 1) // 2
    vec = pl.BlockSpec((1, D), lambda b, i, k: (0, 0))
    per_b = pl.BlockSpec((1, 1, D), lambda b, i, k: (b, 0, 0))
    if n_chunks % 2 == 0:
        first = lambda k: 2 * k
        second = lambda k: 2 * k + 1
    else:
        first = lambda k: jnp.maximum(2 * k - 1, 0)
        second = lambda k: jnp.maximum(2 * k, 2)
    return pl.pallas_call(
        functools.partial(_ffn_kernel, final_norm=final_norm, n_chunks=n_chunks),
        grid=(B, T // tm, nk),
        in_specs=[
            pl.BlockSpec((1, tm, D), lambda b, i, k: (b, i, 0)),
            vec, per_b, per_b, per_b,
            pl.BlockSpec((D, 2 * FF_CHUNK), lambda b, i, k: (0, first(k))),
            pl.BlockSpec((D, 2 * FF_CHUNK), lambda b, i, k: (0, second(k))),
            pl.BlockSpec((FF_CHUNK, D), lambda b, i, k: (first(k), 0)),
            pl.BlockSpec((FF_CHUNK, D), lambda b, i, k: (second(k), 0)),
            vec,
        ],
        out_specs=pl.BlockSpec((1, tm, D), lambda b, i, k: (b, i, 0)),
        out_shape=jax.ShapeDtypeStruct((B, T, D), F32),
        scratch_shapes=[pltpu.VMEM((tm, D), BF16), pltpu.VMEM((tm, D), F32)],
        compiler_params=_cparams(("arbitrary", "arbitrary", "arbitrary")),
    )(x, g, sh, sc, gt, wab, wab, wo, wo, g_final)


GATE_STEPS = N_BRANCH * D_MODEL // PROJ_CHUNK
FIRST_GATE_STEP = 8
DIL_SECTION = {d: d * BRANCH_WIDTH for d in DIL_DILATIONS}


def _rope_apply(z, ra, rb):
    return z * ra + pltpu.roll(z, LANES // 2, 1) * rb


def _inproj_kernel(x_ref, g_ref, sh_ref, sc_ref, w_ref, bg_ref, ra_ref, rb_ref,
                   tok_o, d4_o, d16_o, f_o, nq_o, rest_o, gate_o, h_scr, z4_scr, z16_scr):
    j = pl.program_id(2)
    tm = h_scr.shape[0]
    n_sub = PROJ_CHUNK // MXU_COLS

    def zsub(c):
        return _dot(h_scr[...], w_ref[:, c * MXU_COLS:(c + 1) * MXU_COLS])

    def roped(c):
        z = zsub(c)
        ra = ra_ref[...]
        rb = rb_ref[...]
        return [_rope_apply(z[:, u * LANES:(u + 1) * LANES], ra, rb) for u in range(MXU_COLS // LANES)]

    sub_groups = MXU_COLS // LANES
    dil_groups = BRANCH_WIDTH // LANES

    def tok_step():
        for c in range(n_sub):
            for u, r in enumerate(roped(c)):
                lo = c * MXU_COLS + u * LANES
                tok_o[0, :, lo:lo + LANES] = r.astype(BF16)

    @pl.when(j == 0)
    def _():
        h_scr[...] = _norm_modulate(x_ref[0], g_ref[...], sh_ref[0], sc_ref[0]).astype(BF16)
        tok_step()

    pl.when(j == 2)(tok_step)

    def dil_step(rope):
        for c in range(n_sub):
            parts = roped(c) if rope else [zsub(c)[:, u * LANES:(u + 1) * LANES] for u in range(sub_groups)]
            scr = z4_scr if c < n_sub // 2 else z16_scr
            for u, r in enumerate(parts):
                scr[(c % (n_sub // 2)) * sub_groups + u] = r
        for d, o_ref, scr in ((4, d4_o, z4_scr), (16, d16_o, z16_scr)):
            for r in range(d):
                for u in range(dil_groups):
                    rows = scr[u, pl.ds(r, tm // d, stride=d), :]
                    lo = r * BRANCH_WIDTH + u * LANES
                    o_ref[0, :, lo:lo + LANES] = rows.astype(BF16)

    pl.when((j == 1) | (j == 3))(lambda: dil_step(True))
    pl.when(j == 7)(lambda: dil_step(False))

    @pl.when(j == 4)
    def _():
        for c in range(n_sub):
            o_ref, lo = (f_o, c * MXU_COLS) if c < n_sub // 2 else (nq_o, (c - n_sub // 2) * MXU_COLS)
            o_ref[0, :, lo:lo + MXU_COLS] = zsub(c).astype(BF16)

    @pl.when((j == 5) | (j == 6))
    def _():
        for c in range(n_sub):
            rest_o[0, :, c * MXU_COLS:(c + 1) * MXU_COLS] = zsub(c).astype(BF16)

    @pl.when(j >= FIRST_GATE_STEP)
    def _():
        for c in range(n_sub):
            cols = slice(c * MXU_COLS, (c + 1) * MXU_COLS)
            pre = zsub(c) + bg_ref[:, cols]
            gate_o[0, :, cols] = (0.5 * jnp.tanh(0.5 * pre) + 0.5).astype(BF16)


def _inproj(x, g, sh, sc, w, bg, ra, rb):
    B, T, D = x.shape
    tm = min(TOKEN_TILE, T)
    n_steps = w.shape[1] // PROJ_CHUNK
    vec = pl.BlockSpec((1, D), lambda b, i, j: (0, 0))
    per_b = pl.BlockSpec((1, 1, D), lambda b, i, j: (b, 0, 0))
    tab = pl.BlockSpec((tm, LANES), lambda b, i, j: (i, 0))
    step = lambda j, *edges: sum((j >= e).astype(jnp.int32) for e in edges)
    dil_spec = lambda d: pl.BlockSpec((1, tm // d, DIL_SECTION[d]), lambda b, i, j: (b, i, step(j, 3, 7)))
    tok512 = pl.BlockSpec((1, tm, BRANCH_WIDTH), lambda b, i, j: (b, i, 0))
    return pl.pallas_call(
        _inproj_kernel,
        grid=(B, T // tm, n_steps),
        in_specs=[
            pl.BlockSpec((1, tm, D), lambda b, i, j: (b, i, 0), pipeline_mode=pl.Buffered(1)),
            vec, per_b, per_b,
            pl.BlockSpec((D, PROJ_CHUNK), lambda b, i, j: (0, j)),
            pl.BlockSpec((1, PROJ_CHUNK), lambda b, i, j: (0, jnp.maximum(j - FIRST_GATE_STEP, 0))),
            tab, tab,
        ],
        out_specs=[
            pl.BlockSpec((1, tm, PROJ_CHUNK), lambda b, i, j: (b, i, step(j, 2))),
            dil_spec(4), dil_spec(16),
            tok512, tok512,
            pl.BlockSpec((1, tm, PROJ_CHUNK), lambda b, i, j: (b, i, step(j, 6))),
            pl.BlockSpec((1, tm, PROJ_CHUNK), lambda b, i, j: (b, i, jnp.maximum(j - FIRST_GATE_STEP, 0))),
        ],
        out_shape=[
            jax.ShapeDtypeStruct((B, T, 2 * PROJ_CHUNK), BF16),
            jax.ShapeDtypeStruct((B, T // 4, 3 * DIL_SECTION[4]), BF16),
            jax.ShapeDtypeStruct((B, T // 16, 3 * DIL_SECTION[16]), BF16),
            jax.ShapeDtypeStruct((B, T, BRANCH_WIDTH), BF16),
            jax.ShapeDtypeStruct((B, T, BRANCH_WIDTH), BF16),
            jax.ShapeDtypeStruct((B, T, 2 * PROJ_CHUNK), BF16),
            jax.ShapeDtypeStruct((B, T, GATE_STEPS * PROJ_CHUNK), BF16),
        ],
        scratch_shapes=[pltpu.VMEM((tm, D), BF16)] + [pltpu.VMEM((BRANCH_WIDTH // LANES, tm, LANES), F32)] * 2,
        compiler_params=_cparams(("arbitrary", "arbitrary", "arbitrary")),
    )(x, g, sh, sc, w, bg, ra, rb)


FFT_T2 = 128
FFT_COLS = 2048
FFT_K1_BLOCK = 4


def _fft1_kernel(u_ref, c1_ref, s1_ref, ar_ref, ai_ref):
    u = u_ref[0]
    ar_ref[0] = _dot(c1_ref[...], u).astype(BF16)
    ai_ref[0] = (-_dot(s1_ref[...], u)).astype(BF16)


def _fft2_kernel(ar_ref, ai_ref, twc_ref, tws_ref, c2_ref, s2_ref, cc_ref, sc_ref, o_ref, *, norm):
    c2 = c2_ref[...]
    s2 = s2_ref[...]
    cc = cc_ref[...]
    sc = sc_ref[...]
    for kk in range(FFT_K1_BLOCK):
        rows = slice(kk * FFT_T2, (kk + 1) * FFT_T2)
        ar = ar_ref[0, rows, :].astype(F32)
        ai = ai_ref[0, rows, :].astype(F32)
        twc = jnp.concatenate([twc_ref[rows, :]] * (BRANCH_WIDTH // LANES), axis=1)
        tws = jnp.concatenate([tws_ref[rows, :]] * (BRANCH_WIDTH // LANES), axis=1)
        br = (ar * twc + ai * tws).astype(BF16)
        bi = (ai * twc - ar * tws).astype(BF16)
        zr = _dot(c2, br) + _dot(s2, bi)
        zi = _dot(c2, bi) - _dot(s2, br)
        y = _dot(zr.astype(BF16), cc) + _dot(zi.astype(BF16), sc)
        o_ref[0, :, kk * BRANCH_WIDTH:(kk + 1) * BRANCH_WIDTH] = (y * norm).astype(BF16)


def _dft_tables(T):
    T1 = T // FFT_T2
    k1 = np.arange(T1)
    a1 = 2.0 * np.pi * np.outer(k1, k1) / T1
    k2 = np.arange(FFT_T2)
    a2 = 2.0 * np.pi * np.outer(k2, k2) / FFT_T2
    atw = 2.0 * np.pi * np.outer(k1, k2).reshape(T, 1) / T
    atw = np.broadcast_to(atw, (T, LANES))
    ch = np.arange(FNET_GROUP_DIM)
    ac = 2.0 * np.pi * np.outer(ch, ch) / FNET_GROUP_DIM
    eye = np.eye(BRANCH_WIDTH // FNET_GROUP_DIM)
    bf = lambda a: jnp.asarray(a, dtype=BF16)
    return dict(c1=bf(np.cos(a1)), s1=bf(np.sin(a1)), c2=bf(np.cos(a2)), s2=bf(np.sin(a2)),
                twc=jnp.asarray(np.cos(atw), F32), tws=jnp.asarray(np.sin(atw), F32),
                cc=bf(np.kron(eye, np.cos(ac))), sc=bf(np.kron(eye, np.sin(ac))))


def _fourier_mix(f, tabs):
    B, T, C = f.shape
    T1 = T // FFT_T2
    n_col = FFT_T2 * C // FFT_COLS
    full2 = lambda shape: pl.BlockSpec(shape, lambda b, i: (0, 0))
    ar, ai = pl.pallas_call(
        _fft1_kernel,
        grid=(B, n_col),
        in_specs=[pl.BlockSpec((1, T1, FFT_COLS), lambda b, i: (b, 0, i)),
                  full2((T1, T1)), full2((T1, T1))],
        out_specs=[pl.BlockSpec((1, T1, FFT_COLS), lambda b, i: (b, 0, i))] * 2,
        out_shape=[jax.ShapeDtypeStruct((B, T1, FFT_T2 * C), BF16)] * 2,
        compiler_params=_cparams(("arbitrary", "arbitrary")),
    )(f.reshape(B, T1, FFT_T2 * C), tabs["c1"], tabs["s1"])
    ar = ar.reshape(B, T, C)
    ai = ai.reshape(B, T, C)
    rows = FFT_K1_BLOCK * FFT_T2
    y = pl.pallas_call(
        functools.partial(_fft2_kernel, norm=1.0 / math.sqrt(T * FNET_GROUP_DIM)),
        grid=(B, T1 // FFT_K1_BLOCK),
        in_specs=[pl.BlockSpec((1, rows, C), lambda b, i: (b, i, 0)),
                  pl.BlockSpec((1, rows, C), lambda b, i: (b, i, 0)),
                  pl.BlockSpec((rows, LANES), lambda b, i: (i, 0)),
                  pl.BlockSpec((rows, LANES), lambda b, i: (i, 0)),
                  full2((FFT_T2, FFT_T2)), full2((FFT_T2, FFT_T2)),
                  full2((C, C)), full2((C, C))],
        out_specs=pl.BlockSpec((1, FFT_T2, FFT_K1_BLOCK * C), lambda b, i: (b, 0, i)),
        out_shape=jax.ShapeDtypeStruct((B, FFT_T2, T1 * C), BF16),
        compiler_params=_cparams(("arbitrary", "arbitrary")),
    )(ar, ai, tabs["twc"], tabs["tws"], tabs["c2"], tabs["s2"], tabs["cc"], tabs["sc"])
    return y.reshape(B, T, C)


NAT_Q_ROWS = 8
NAT_Q_TOK = NAT_Q_ROWS * GRID_W
NAT_EDGE_TOK = (NAT_WIN_H // 2) * GRID_W
NAT_WIN_TOK = NAT_Q_TOK + 2 * NAT_EDGE_TOK
NAT_SUB_ROWS = NAT_Q_ROWS // 2
NAT_SUB_TOK = NAT_SUB_ROWS * GRID_W
NAT_SUB_WIN = (NAT_SUB_ROWS + NAT_WIN_H) * GRID_W


def _head_masks():
    lane = lax.broadcasted_iota(jnp.int32, (1, LANES), 1)
    return lane < HEAD_DIM


def _nat_kernel(q_ref, kp_ref, kc_ref, kn_ref, vp_ref, vc_ref, vn_ref, bias0_ref, bias1_ref, o_ref, kw, vw):
    kw[0:NAT_EDGE_TOK, :] = kp_ref[0]
    kw[NAT_EDGE_TOK:NAT_EDGE_TOK + NAT_Q_TOK, :] = kc_ref[0]
    kw[NAT_EDGE_TOK + NAT_Q_TOK:, :] = kn_ref[0]
    vw[0:NAT_EDGE_TOK, :] = vp_ref[0]
    vw[NAT_EDGE_TOK:NAT_EDGE_TOK + NAT_Q_TOK, :] = vc_ref[0]
    vw[NAT_EDGE_TOK + NAT_Q_TOK:, :] = vn_ref[0]
    first = _head_masks()
    ones = jnp.ones((NAT_SUB_WIN, LANES), BF16)
    for t, bias_ref in enumerate((bias0_ref, bias1_ref)):
        rows = slice(t * NAT_SUB_TOK, (t + 1) * NAT_SUB_TOK)
        wrows = slice(t * NAT_SUB_TOK, t * NAT_SUB_TOK + NAT_SUB_WIN)
        for hp in range(NAT_HEADS // 2):
            cols = slice(hp * LANES, (hp + 1) * LANES)
            q = q_ref[0, rows, cols]
            k = kw[wrows, cols]
            v1 = jnp.concatenate([vw[wrows, cols], ones], axis=1)
            outs = []
            for hh in range(2):
                sel = first if hh == 0 else jnp.logical_not(first)
                qm = jnp.where(sel, q, jnp.zeros_like(q))
                s = _dot_nt(qm, k) + bias_ref[0, 2 * hp + hh]
                m = jnp.max(s, axis=-1, keepdims=True)
                ov = _dot(jnp.exp2(s - m).astype(BF16), v1)
                outs.append(ov[:, :LANES] / ov[:, LANES:])
            o_ref[0, rows, cols] = jnp.where(first, outs[0], outs[1]).astype(BF16)


def _nat_bias_table(rel_bias):
    H = rel_bias.shape[0]
    half = NAT_WIN_H // 2
    col = np.arange(GRID_W)
    col_start = np.clip(col - NAT_WIN_W // 2, 0, GRID_W - NAT_WIN_W)
    kc = np.arange(GRID_W)
    col_ok = (kc[None, :] >= col_start[:, None]) & (kc[None, :] < col_start[:, None] + NAT_WIN_W)
    col_off = np.clip(kc[None, :] - col[:, None] + (NAT_WIN_W - 1), 0, 2 * NAT_WIN_W - 2)
    rr = np.arange(NAT_SUB_ROWS)
    wr = np.arange(NAT_SUB_ROWS + NAT_WIN_H)
    start = np.stack([np.full_like(rr, half), rr, np.zeros_like(rr)])
    row_ok = (wr[None, None, :] >= start[:, :, None]) & (wr[None, None, :] < start[:, :, None] + NAT_WIN_H)
    row_off = np.clip(wr[None, :] - half - rr[:, None] + (NAT_WIN_H - 1), 0, 2 * NAT_WIN_H - 2)
    tab = rel_bias[:, row_off][:, :, :, col_off]
    ok = row_ok[:, None, :, :, None, None] & col_ok[None, None, None, None]
    tab = jnp.where(jnp.asarray(ok), tab.astype(F32)[None] * LOG2E, NEG_INF)
    tab = jnp.transpose(tab, (0, 1, 2, 4, 3, 5))
    return tab.reshape(3, H, NAT_SUB_TOK, NAT_SUB_WIN)


def _neighborhood_attention(nq, rest, bias_tab):
    B, T, C = nq.shape
    n_blk = T // NAT_Q_TOK
    kind0 = lambda i: 1 - (i == 0).astype(jnp.int32)
    kind1 = lambda i: 1 + (i == n_blk - 1).astype(jnp.int32)
    per = NAT_Q_TOK // NAT_EDGE_TOK
    n_edge = T // NAT_EDGE_TOK
    prev = lambda c: (lambda b, i: (b, jnp.maximum(i * per - 1, 0), c))
    cur = lambda c: (lambda b, i: (b, i, c))
    nxt = lambda c: (lambda b, i: (b, jnp.minimum((i + 1) * per, n_edge - 1), c))
    edge = lambda f: pl.BlockSpec((1, NAT_EDGE_TOK, C), f)
    mid = lambda f: pl.BlockSpec((1, NAT_Q_TOK, C), f)
    bias = lambda kind: pl.BlockSpec((1,) + bias_tab.shape[1:], lambda b, i: (kind(i), 0, 0, 0),
                                     pipeline_mode=pl.Buffered(1))
    return pl.pallas_call(
        _nat_kernel,
        grid=(B, n_blk),
        in_specs=[mid(cur(0)),
                  edge(prev(0)), mid(cur(0)), edge(nxt(0)),
                  edge(prev(1)), mid(cur(1)), edge(nxt(1)),
                  bias(kind0), bias(kind1)],
        out_specs=mid(cur(0)),
        out_shape=jax.ShapeDtypeStruct((B, T, C), BF16),
        scratch_shapes=[pltpu.VMEM((NAT_WIN_TOK, C), BF16), pltpu.VMEM((NAT_WIN_TOK, C), BF16)],
        compiler_params=_cparams(("arbitrary", "arbitrary")),
    )(nq, rest, rest, rest, rest, rest, rest, bias_tab, bias_tab)


DIFF_TQ = 1024
DIFF_TK = 512
DIFF_UNROLL = 16
DIFF_HEADROOM = 64.0


def _qk_head_mask():
    lane = lax.broadcasted_iota(jnp.int32, (1, LANES), 1)
    return (lane % (LANES // 2)) < HEAD_DIM // 2


def _diff_kernel(q_ref, k_ref, v_ref, lq1_ref, lk1_ref, lq2_ref, lk2_ref, g_ref, o_ref, a1_scr, a2_scr,
                 *, lam_init, n_kv):
    first = _qk_head_mask()
    q = q_ref[0]
    zero = jnp.zeros_like(q)
    q1 = jnp.where(first, q, zero)
    q2 = jnp.where(first, zero, q)
    tq = q.shape[0]
    ones = jnp.ones((DIFF_TK, LANES), BF16)
    zacc = jnp.zeros((tq, 2 * LANES), F32)

    def tile(c):
        k0 = pl.multiple_of(c * DIFF_TK, DIFF_TK)
        k = k_ref[0, pl.ds(k0, DIFF_TK), :]
        v1 = jnp.concatenate([v_ref[0, pl.ds(k0, DIFF_TK), :], ones], axis=1)
        return k, v1

    k, _ = tile(0)
    r1 = jnp.max(_dot_nt(q1, k), axis=-1, keepdims=True)
    r2 = jnp.max(_dot_nt(q2, k), axis=-1, keepdims=True)

    def fast(c, carry):
        a1, a2, t1, t2 = carry
        k, v1 = tile(c)

        def one(qm, r, a, t):
            s = _dot_nt(qm, k)
            for u in range(DIFF_TK // LANES):
                t = jnp.maximum(t, s[:, u * LANES:(u + 1) * LANES])
            return a + _dot(jnp.exp2(s - r).astype(BF16), v1), t

        a1, t1 = one(q1, r1, a1, t1)
        a2, t2 = one(q2, r2, a2, t2)
        return a1, a2, t1, t2

    tneg = jnp.full((tq, LANES), NEG_INF, F32)
    a1, a2, t1, t2 = lax.fori_loop(0, n_kv, fast, (zacc, zacc, tneg, tneg), unroll=DIFF_UNROLL)
    a1_scr[...] = a1
    a2_scr[...] = a2
    growth = jnp.max(jnp.maximum(t1 - r1, t2 - r2))

    @pl.when(jnp.logical_not(growth <= DIFF_HEADROOM))
    def _():
        def slow(c, carry):
            m1, b1, m2, b2 = carry
            k, v1 = tile(c)

            def one(qm, m, a):
                s = _dot_nt(qm, k)
                mn = jnp.maximum(m, jnp.max(s, axis=-1, keepdims=True))
                return mn, jnp.exp2(m - mn) * a + _dot(jnp.exp2(s - mn).astype(BF16), v1)

            m1, b1 = one(q1, m1, b1)
            m2, b2 = one(q2, m2, b2)
            return m1, b1, m2, b2

        neg = jnp.full((tq, 1), NEG_INF, F32)
        _, b1, _, b2 = lax.fori_loop(0, n_kv, slow, (neg, zacc, neg, zacc))
        a1_scr[...] = b1
        a2_scr[...] = b2

    lam = (jnp.exp(jnp.sum(lq1_ref[...] * lk1_ref[...], keepdims=True))
           - jnp.exp(jnp.sum(lq2_ref[...] * lk2_ref[...], keepdims=True)) + lam_init)
    o = (a1_scr[:, :LANES] / a1_scr[:, LANES:]) - lam * (a2_scr[:, :LANES] / a2_scr[:, LANES:])
    ms = jnp.mean(o * o, axis=-1, keepdims=True)
    o = o * lax.rsqrt(ms + EPS) * g_ref[...]
    o_ref[0] = (o * (1.0 - lam_init)).astype(BF16)


def _diff_attention(tok, rest, lq1, lk1, lq2, lk2, ln_g, lam_init):
    B, T, _ = tok.shape
    tq = min(DIFF_TQ, T)
    k_blk = 1024 // LANES
    v_blk = 1024 // LANES
    vec = lambda n: pl.BlockSpec((1, n), lambda b, h, i: (0, 0))
    return pl.pallas_call(
        functools.partial(_diff_kernel, lam_init=lam_init, n_kv=T // DIFF_TK),
        grid=(B, DIFF_HEADS, T // tq),
        in_specs=[pl.BlockSpec((1, tq, LANES), lambda b, h, i: (b, i, h)),
                  pl.BlockSpec((1, T, LANES), lambda b, h, i: (b, 0, k_blk + h)),
                  pl.BlockSpec((1, T, LANES), lambda b, h, i: (b, 0, v_blk + h)),
                  vec(HEAD_DIM), vec(HEAD_DIM), vec(HEAD_DIM), vec(HEAD_DIM), vec(LANES)],
        out_specs=pl.BlockSpec((1, tq, LANES), lambda b, h, i: (b, i, h)),
        out_shape=jax.ShapeDtypeStruct((B, T, DIFF_HEADS * LANES), BF16),
        scratch_shapes=[pltpu.VMEM((tq, 2 * LANES), F32), pltpu.VMEM((tq, 2 * LANES), F32)],
        compiler_params=_cparams(("arbitrary", "arbitrary", "arbitrary")),
    )(tok, tok, rest, lq1, lk1, lq2, lk2, ln_g)


DIL_TQ = 1024
DIL_SUB = 128
DIL_EDGE = DIL_SIDE


def _dil_kernel(q_ref, kp_ref, kc_ref, kn_ref, vp_ref, vc_ref, vn_ref, o_ref, lse_ref, kw, vw, *, seq, tq):
    i = pl.program_id(2)
    kw[0:DIL_EDGE, :] = kp_ref[0]
    kw[DIL_EDGE:DIL_EDGE + tq, :] = kc_ref[0]
    kw[DIL_EDGE + tq:, :] = kn_ref[0]
    vw[0:DIL_EDGE, :] = vp_ref[0]
    vw[DIL_EDGE:DIL_EDGE + tq, :] = vc_ref[0]
    vw[DIL_EDGE + tq:, :] = vn_ref[0]
    sub = min(DIL_SUB, tq)
    win = sub + 2 * DIL_EDGE
    first_qk = _qk_head_mask()
    first_v = _head_masks()
    ones = jnp.ones((win, LANES), BF16)
    for t in range(tq // sub):
        s0 = i * tq + t * sub
        qpos = s0 + lax.broadcasted_iota(jnp.int32, (sub, win), 0)
        kpos = s0 - DIL_EDGE + lax.broadcasted_iota(jnp.int32, (sub, win), 1)
        valid = (jnp.abs(kpos - qpos) <= DIL_SIDE) & (kpos >= 0) & (kpos < seq)
        rows = slice(t * sub, (t + 1) * sub)
        wrows = slice(t * sub, t * sub + win)
        for hp in range(DIL_HEADS // 2):
            cols = slice(hp * LANES, (hp + 1) * LANES)
            q = q_ref[0, rows, cols]
            k = kw[wrows, cols]
            v1 = jnp.concatenate([vw[wrows, cols], ones], axis=1)
            outs, lses = [], []
            for hh in range(2):
                sel = first_qk if hh == 0 else jnp.logical_not(first_qk)
                qm = jnp.where(sel, q, jnp.zeros_like(q))
                s = jnp.where(valid, _dot_nt(qm, k), NEG_INF)
                m = jnp.max(s, axis=-1, keepdims=True)
                ov = _dot(jnp.exp2(s - m).astype(BF16), v1)
                l = ov[:, LANES:]
                outs.append(ov[:, :LANES] / l)
                lses.append(m + jnp.log2(l))
            o_ref[0, rows, cols] = jnp.where(first_v, outs[0], outs[1]).astype(BF16)
            lse_ref[0, rows, cols] = jnp.where(first_v, lses[0], lses[1])


def _dilated_group(qa, ka, va, qc, kc, vc, dil):
    B, seq, _ = qa.shape
    C = DIL_HEADS * HEAD_DIM
    tq = min(DIL_TQ, seq)
    per = tq // DIL_EDGE
    n_edge = seq // DIL_EDGE
    prev = lambda cf: (lambda b, r, i: (b, jnp.maximum(i * per - 1, 0), cf(r)))
    cur = lambda cf: (lambda b, r, i: (b, i, cf(r)))
    nxt = lambda cf: (lambda b, r, i: (b, jnp.minimum((i + 1) * per, n_edge - 1), cf(r)))
    edge = lambda f: pl.BlockSpec((1, DIL_EDGE, C), f)
    mid = lambda f: pl.BlockSpec((1, tq, C), f)
    out_map = lambda b, r, i: (b, i, r)
    return pl.pallas_call(
        functools.partial(_dil_kernel, seq=seq, tq=tq),
        grid=(B, dil, seq // tq),
        in_specs=[mid(cur(qc)),
                  edge(prev(kc)), mid(cur(kc)), edge(nxt(kc)),
                  edge(prev(vc)), mid(cur(vc)), edge(nxt(vc))],
        out_specs=[pl.BlockSpec((1, tq, C), out_map), pl.BlockSpec((1, tq, C), out_map)],
        out_shape=[jax.ShapeDtypeStruct((B, seq, dil * C), BF16),
                   jax.ShapeDtypeStruct((B, seq, dil * C), F32)],
        scratch_shapes=[pltpu.VMEM((tq + 2 * DIL_EDGE, C), BF16), pltpu.VMEM((tq + 2 * DIL_EDGE, C), BF16)],
        compiler_params=_cparams(("arbitrary", "arbitrary", "arbitrary")),
    )(qa, ka, ka, ka, va, va, va)


MERGE_CHUNK = 512


def _merge_kernel(x_ref, gt_ref, ya_ref, yb_ref, yc_ref, o0_ref, o1_ref, o2_ref, s0_ref, s1_ref, s2_ref,
                  g_ref, wb_ref, wo_ref, out_ref, o1_scr, s1_scr, o2_scr, s2_scr):
    tm = out_ref.shape[1]
    C = BRANCH_WIDTH
    D = out_ref.shape[2]
    n_j = D // MERGE_CHUNK
    for d, o_ref, s_ref, o_scr, s_scr in ((4, o1_ref, s1_ref, o1_scr, s1_scr),
                                          (16, o2_ref, s2_ref, o2_scr, s2_scr)):
        for r in range(d):
            for u in range(C // LANES):
                cols = slice(r * C + u * LANES, r * C + (u + 1) * LANES)
                o_scr[u, pl.ds(r, tm // d, stride=d), :] = o_ref[0, :, cols].astype(F32)
                s_scr[u, pl.ds(r, tm // d, stride=d), :] = s_ref[0, :, cols]
    wide = lambda scr: jnp.concatenate([scr[u] for u in range(C // LANES)], axis=1)
    s0 = s0_ref[0]
    s1 = wide(s1_scr)
    s2 = wide(s2_scr)
    m = jnp.maximum(jnp.maximum(s0, s1), s2)
    e0 = jnp.exp2(s0 - m)
    e1 = jnp.exp2(s1 - m)
    e2 = jnp.exp2(s2 - m)
    num = o0_ref[0].astype(F32) * e0 + wide(o1_scr) * e1 + wide(o2_scr) * e2
    ys = [ya_ref[0], yb_ref[0], yc_ref[0], (num / (e0 + e1 + e2)).astype(BF16)]

    acc = None
    for j in range(n_j):
        merged = None
        for n in range(N_BRANCH):
            lo = n * D + j * MERGE_CHUNK
            term = g_ref[0, :, lo:lo + MERGE_CHUNK].astype(F32) * _dot(ys[n], wb_ref[j, n])
            merged = term if merged is None else merged + term
        part = _dot(merged.astype(BF16), wo_ref[j])
        acc = part if acc is None else acc + part
    out_ref[0] = x_ref[0] + gt_ref[0] * acc


def _merge(x, gt, ya, yb, yc, dil_o, dil_lse, gates, wb, wo):
    B, T, D = x.shape
    tm = min(MERGE_TILE, T)
    C = BRANCH_WIDTH
    n_j = D // MERGE_CHUNK
    tok = pl.BlockSpec((1, tm, C), lambda b, i: (b, i, 0))
    res = lambda d: pl.BlockSpec((1, tm // d, d * C), lambda b, i: (b, i, 0))
    dil_specs = [res(d) for d in DIL_DILATIONS]
    return pl.pallas_call(
        _merge_kernel,
        grid=(B, T // tm),
        in_specs=[pl.BlockSpec((1, tm, D), lambda b, i: (b, i, 0)),
                  pl.BlockSpec((1, 1, D), lambda b, i: (b, 0, 0)),
                  tok, tok, tok, *dil_specs, *dil_specs,
                  pl.BlockSpec((1, tm, N_BRANCH * D), lambda b, i: (b, i, 0)),
                  pl.BlockSpec((n_j, N_BRANCH, C, MERGE_CHUNK), lambda b, i: (0, 0, 0, 0),
                               pipeline_mode=pl.Buffered(1)),
                  pl.BlockSpec((n_j, MERGE_CHUNK, D), lambda b, i: (0, 0, 0),
                               pipeline_mode=pl.Buffered(1))],
        out_specs=pl.BlockSpec((1, tm, D), lambda b, i: (b, i, 0)),
        out_shape=jax.ShapeDtypeStruct((B, T, D), F32),
        scratch_shapes=[pltpu.VMEM((C // LANES, tm, LANES), F32)] * 4,
        compiler_params=_cparams(("arbitrary", "arbitrary")),
    )(x, gt, ya, yb, yc, *dil_o, *dil_lse, gates, wb, wo)


def _prep_ffn(w_in, w_out):
    D = w_in.shape[0]
    pad = D_FF_PAD - D_FF
    a = jnp.pad(w_in[:, :D_FF].astype(BF16), ((0, 0), (0, pad)))
    b = jnp.pad(w_in[:, D_FF:].astype(BF16), ((0, 0), (0, pad)))
    n = D_FF_PAD // FF_CHUNK
    wab = jnp.concatenate([a.reshape(D, n, FF_CHUNK), b.reshape(D, n, FF_CHUNK)], axis=2)
    wo = jnp.pad(w_out.astype(BF16), ((0, pad), (0, 0)))
    return wab.reshape(D, n * 2 * FF_CHUNK), wo


def _qk_lane_order():
    rot_half = ROT_DIM // 2
    rest_half = (HEAD_DIM - ROT_DIM) // 2
    idx = []
    for n in range(LANES):
        half, w = divmod(n, LANES // 2)
        head, i = divmod(w, HEAD_DIM // 2)
        d = i + rot_half * half if i < rot_half else ROT_DIM + (i - rot_half) + rest_half * half
        idx.append(head * HEAD_DIM + d)
    return np.asarray(idx)


def _prep_w_in(w):
    W = BRANCH_WIDTH
    D = w.shape[0]
    qscale = HEAD_DIM ** -0.5 * LOG2E
    order = _qk_lane_order()
    qk = lambda cols: cols.reshape(D, -1, LANES)[:, :, order].reshape(D, -1)
    f_in, nq, nk, nv, dq, dk, dv = [w[:, n * W:(n + 1) * W] for n in range(7)]
    lq = [qk(w[:, (7 + g) * W:(8 + g) * W] * qscale) for g in range(3)]
    lk = [qk(w[:, (10 + g) * W:(11 + g) * W]) for g in range(3)]
    lv = [w[:, (13 + g) * W:(14 + g) * W] for g in range(3)]
    gates = w[:, MIX_COLS:]
    cols = [qk(dq * qscale), lq[0], lq[1], lq[2], qk(dk), lk[0], lk[1], lk[2],
            f_in, nq * qscale, nk, nv, dv, lv[0], lv[1], lv[2], gates]
    return jnp.concatenate(cols, axis=1).astype(BF16)


def _rope_tables(T):
    rot_half = ROT_DIM // 2
    inv = ROPE_THETA ** (-jnp.arange(0, ROT_DIM, 2, dtype=F32) / ROT_DIM)
    ang = jnp.arange(T, dtype=F32)[:, None] * inv[None, :]
    cos, sin = jnp.cos(ang), jnp.sin(ang)
    lane = np.arange(LANES)
    i = lane % (HEAD_DIM // 2)
    is_rot = jnp.asarray(i < rot_half)[None, :]
    src = np.minimum(i, rot_half - 1)
    sign = jnp.asarray(np.where(lane < LANES // 2, -1.0, 1.0), F32)[None, :]
    ra = jnp.where(is_rot, cos[:, src], 1.0)
    rb = jnp.where(is_rot, sin[:, src] * sign, 0.0)
    return ra, rb


def _trunk(x, mods, P, tabs):
    B, T, D = x.shape
    for l in range(DEPTH):
        sh1, sc1, gt1, sh2, sc2, gt2, sh3, sc3, gt3 = [mods[l][:, i:i + 1, :] for i in range(N_MOD)]
        x = _ffn(x, P["g_ffn1"][l], sh1, sc1, gt1, P["wab1"][l], P["wo1"][l], P["g_final"], False)
        tok, d4, d16, f_in, nq, rest, gates = _inproj(x, P["g_mix"][l], sh2, sc2, P["w_in"][l],
                                                      P["b_gate"][l], *tabs["rope"])
        ya = _fourier_mix(f_in, tabs["dft"])
        yb = _neighborhood_attention(nq, rest, P["nat_bias"][l])
        lam_init = 0.8 - 0.6 * math.exp(-0.3 * l)
        yc = _diff_attention(tok, rest, P["lam_q1"][l], P["lam_k1"][l], P["lam_q2"][l], P["lam_k2"][l],
                             P["diff_ln_g"][l], lam_init)
        dil = [_dilated_group(tok, tok, rest, lambda r: 1, lambda r: 3, lambda r: 3, 1),
               _dilated_group(d4, d4, d4, lambda r: r, lambda r: 4 + r, lambda r: 8 + r, 4),
               _dilated_group(d16, d16, d16, lambda r: r, lambda r: 16 + r, lambda r: 32 + r, 16)]
        x = _merge(x, gt2, ya, yb, yc, [o for o, _ in dil], [s for _, s in dil], gates,
                   P["w_branch"][l], P["w_out"][l])
        x = _ffn(x, P["g_ffn2"][l], sh3, sc3, gt3, P["wab2"][l], P["wo2"][l], P["g_final"], l == DEPTH - 1)
    return x


def kernel(x_prompt, x_sample, c_prompt, c_sample, w_ada, b_ada, g_ffn1, w_ffn1_in, w_ffn1_out, g_mix, w_in, b_gate, nat_rel_bias, lam_q1, lam_k1, lam_q2, lam_k2, diff_ln_g, w_branch, w_out, g_ffn2, w_ffn2_in, w_ffn2_out, g_final):
    L = DEPTH
    D = D_MODEL
    Bp, Bs = c_prompt.shape[0], c_sample.shape[0]
    rows = ((Bp + Bs + 7) // 8) * 8
    c_all = jnp.concatenate([c_prompt, c_sample, jnp.zeros((rows - Bp - Bs, D), F32)], axis=0)
    mods = _modulation(c_all, w_ada, b_ada).reshape(L, rows, N_MOD, D)

    P = {}
    ffn1 = [_prep_ffn(w_ffn1_in[l], w_ffn1_out[l]) for l in range(L)]
    ffn2 = [_prep_ffn(w_ffn2_in[l], w_ffn2_out[l]) for l in range(L)]
    P["wab1"] = [a for a, _ in ffn1]
    P["wo1"] = [b for _, b in ffn1]
    P["wab2"] = [a for a, _ in ffn2]
    P["wo2"] = [b for _, b in ffn2]
    P["w_in"] = [_prep_w_in(w_in[l]) for l in range(L)]
    P["b_gate"] = [b_gate[l].reshape(1, N_BRANCH * D) for l in range(L)]
    P["nat_bias"] = [_nat_bias_table(nat_rel_bias[l]) for l in range(L)]
    n_j = D // MERGE_CHUNK
    P["w_branch"] = [jnp.transpose(w_branch[l].astype(BF16).reshape(N_BRANCH, BRANCH_WIDTH, n_j, MERGE_CHUNK),
                                   (2, 0, 1, 3)) for l in range(L)]
    P["w_out"] = [w_out[l].astype(BF16).reshape(n_j, MERGE_CHUNK, D) for l in range(L)]
    for name, arr in (("g_ffn1", g_ffn1), ("g_mix", g_mix), ("g_ffn2", g_ffn2), ("lam_q1", lam_q1),
                      ("lam_k1", lam_k1), ("lam_q2", lam_q2), ("lam_k2", lam_k2), ("diff_ln_g", diff_ln_g)):
        P[name] = [arr[l].reshape(1, -1) for l in range(L)]
    P["g_final"] = g_final.reshape(1, D)

    outs = []
    for x, lo, hi in ((x_prompt, 0, Bp), (x_sample, Bp, Bp + Bs)):
        T = x.shape[1]
        tabs = {"rope": _rope_tables(T), "dft": _dft_tables(T)}
        outs.append(_trunk(x, mods[:, lo:hi], P, tabs))
    return tuple(outs)
```
